```python
import jax, jax.numpy as jnp
from jax import lax
import numpy as np


D_MODEL = 1024
BATCH = 32
SEQ = 2048
DEPTH = 2

CHUNK = 64
N_META = 16
EPS = 1e-6
N_EVEN = (DEPTH + 1) // 2
N_ODD = DEPTH // 2
CONV_W = 4

LRU_WIDTH = D_MODEL
LRU_BLOCKS = 4
LRU_BLOCK = LRU_WIDTH // LRU_BLOCKS
RG_LRU_C = 8.0

SSD_WIDTH = D_MODEL
SSD_HEAD_DIM = 64
SSD_HEADS = SSD_WIDTH // SSD_HEAD_DIM
SSD_GROUPS = 2
SSD_HPG = SSD_HEADS // SSD_GROUPS
SSD_STATE = 128
SSD_CHUNK = CHUNK
SSD_CONV_DIM = SSD_WIDTH + 2 * SSD_GROUPS * SSD_STATE

EVEN_SPLITS = (LRU_WIDTH, 2 * LRU_WIDTH, 2 * LRU_WIDTH + SSD_WIDTH,
               2 * LRU_WIDTH + SSD_WIDTH + SSD_CONV_DIM)
EVEN_IN = 2 * LRU_WIDTH + SSD_WIDTH + SSD_CONV_DIM + SSD_HEADS
EVEN_MIX = LRU_WIDTH + SSD_WIDTH

SB_HEADS = 16
SB_HEAD_DIM = D_MODEL // SB_HEADS
SB_WIDTH = SB_HEADS * SB_HEAD_DIM
SB_BLOCK = 128
ODD_IN = 4 * SB_WIDTH

kernel_name = 'hybrid_rglru_ssd_stickbreaking_meta'


def rmsnorm(x, w):
    xf = x.astype(jnp.float32)
    y = xf * lax.rsqrt(jnp.mean(xf * xf, axis=-1, keepdims=True) + EPS)
    return (y * w.astype(jnp.float32)).astype(x.dtype)


def causal_dwconv(u, w, b):
    out = lax.conv_general_dilated(
        u, w[:, None, :].astype(u.dtype), window_strides=(1,),
        padding=[(CONV_W - 1, 0)], dimension_numbers=('NWC', 'WIO', 'NWC'),
        feature_group_count=u.shape[-1])
    return out + b


def linear_scan(a, b):
    def combine(left, right):
        al, bl = left
        ar, br = right
        return al * ar, ar * bl + br
    _, h = lax.associative_scan(combine, (a, b), axis=1)
    return h


def rg_lru(u, w_a, b_a, w_x, b_x, lam):
    bsz, L, _ = u.shape
    uf = u.astype(jnp.float32)
    ub = uf.reshape(bsz, L, LRU_BLOCKS, LRU_BLOCK)
    r = jax.nn.sigmoid(jnp.einsum('blgi,gij->blgj', ub, w_a).reshape(bsz, L, LRU_WIDTH) + b_a)
    i = jax.nn.sigmoid(jnp.einsum('blgi,gij->blgj', ub, w_x).reshape(bsz, L, LRU_WIDTH) + b_x)
    log_a = -RG_LRU_C * r * jax.nn.softplus(-lam)
    a = jnp.exp(log_a)
    mult = jnp.sqrt(-jnp.expm1(2.0 * log_a))
    return linear_scan(a, mult * i * uf)


def ssd_scan(xh, dt, a, bmat, cmat):
    bsz, L, _, _ = xh.shape
    pad = (-L) % SSD_CHUNK
    def padf(t):
        return jnp.pad(t, [(0, 0), (pad, 0)] + [(0, 0)] * (t.ndim - 2))
    f32 = jnp.float32
    xdt = padf((xh * dt[..., None]).astype(f32))
    adt = padf((dt * a).astype(f32))
    bm = padf(bmat.astype(f32))
    cm = padf(cmat.astype(f32))
    nc = (L + pad) // SSD_CHUNK
    X = xdt.reshape(bsz, nc, SSD_CHUNK, SSD_GROUPS, SSD_HPG, SSD_HEAD_DIM)
    A = adt.reshape(bsz, nc, SSD_CHUNK, SSD_GROUPS, SSD_HPG).transpose(0, 3, 4, 1, 2)
    Bc = bm.reshape(bsz, nc, SSD_CHUNK, SSD_GROUPS, SSD_STATE)
    Cc = cm.reshape(bsz, nc, SSD_CHUNK, SSD_GROUPS, SSD_STATE)
    a_cum = jnp.cumsum(A, axis=-1)
    tri = jnp.tril(jnp.ones((SSD_CHUNK, SSD_CHUNK), bool))
    seg = a_cum[..., :, None] - a_cum[..., None, :]
    decay = jnp.exp(jnp.where(tri, seg, -jnp.inf))
    cb = jnp.einsum('bclgn,bcsgn->bcgls', Cc, Bc)
    y_diag = jnp.einsum('bcgls,bgecls,bcsgep->bclgep', cb, decay, X)
    decay_states = jnp.exp(a_cum[..., -1:] - a_cum)
    states = jnp.einsum('bclgn,bgecl,bclgep->bcgepn', Bc, decay_states, X)
    chunk_tot = jnp.pad(a_cum[..., -1], [(0, 0), (0, 0), (0, 0), (1, 0)])
    cs = jnp.cumsum(chunk_tot, axis=-1)
    tri_c = jnp.tril(jnp.ones((nc + 1, nc + 1), bool))
    decay_chunk = jnp.exp(jnp.where(tri_c, cs[..., :, None] - cs[..., None, :], -jnp.inf))
    states = jnp.concatenate([jnp.zeros_like(states[:, :1]), states], axis=1)
    states = jnp.einsum('bgezc,bcgepn->bzgepn', decay_chunk, states)[:, :-1]
    y_off = jnp.einsum('bclgn,bcgepn,bgecl->bclgep', Cc, states, jnp.exp(a_cum))
    y = (y_diag + y_off).reshape(bsz, L + pad, SSD_HEADS, SSD_HEAD_DIM)
    return y[:, pad:]


def gated_group_rmsnorm(y, z, w):
    bsz, L, W = y.shape
    g = (y * jax.nn.silu(z.astype(jnp.float32))).reshape(bsz, L, SSD_GROUPS, W // SSD_GROUPS)
    g = g * lax.rsqrt(jnp.mean(g * g, axis=-1, keepdims=True) + EPS)
    return g.reshape(bsz, L, W) * w.astype(jnp.float32)


def rglru_ssd_layer(h, norm_w, w_in, lru_conv_w, lru_conv_b, lru_w_a, lru_b_a,
                    lru_w_x, lru_b_x, lru_lambda, ssd_conv_w, ssd_conv_b,
                    ssd_dt_bias, ssd_a_log, ssd_d, ssd_norm, w_out):
    bsz, L, _ = h.shape
    u = rmsnorm(h, norm_w)
    proj = u @ w_in
    lru_x, lru_g, ssd_z, ssd_xbc, ssd_dt = jnp.split(proj, EVEN_SPLITS, axis=-1)
    lx = causal_dwconv(lru_x, lru_conv_w, lru_conv_b)
    y_a = rg_lru(lx, lru_w_a, lru_b_a, lru_w_x, lru_b_x, lru_lambda) * jax.nn.silu(lru_g.astype(jnp.float32))
    xbc = jax.nn.silu(causal_dwconv(ssd_xbc, ssd_conv_w, ssd_conv_b))
    xs, bm, cm = jnp.split(xbc, (SSD_WIDTH, SSD_WIDTH + SSD_GROUPS * SSD_STATE), axis=-1)
    dt = jax.nn.softplus(ssd_dt.astype(jnp.float32) + ssd_dt_bias)
    a = -jnp.exp(ssd_a_log.astype(jnp.float32))
    xh = xs.reshape(bsz, L, SSD_HEADS, SSD_HEAD_DIM)
    y = ssd_scan(xh, dt, a,
                 bm.reshape(bsz, L, SSD_GROUPS, SSD_STATE),
                 cm.reshape(bsz, L, SSD_GROUPS, SSD_STATE))
    y = y + xh.astype(jnp.float32) * ssd_d[:, None]
    y_b = gated_group_rmsnorm(y.reshape(bsz, L, SSD_WIDTH), ssd_z, ssd_norm)
    mixed = jnp.concatenate([y_a, y_b], axis=-1).astype(h.dtype)
    return h + mixed @ w_out


def stick_breaking_block(q_blk, k_ctx, v_ctx, q0):
    tq = q_blk.shape[1]
    s_len = k_ctx.shape[1]
    z = jnp.einsum('bthd,bshd->bhts', q_blk.astype(jnp.float32),
                   k_ctx.astype(jnp.float32)) * (SB_HEAD_DIM ** -0.5)
    before = jnp.arange(s_len)[None, :] < (q0 + jnp.arange(tq))[:, None]
    log_keep = jnp.where(before, jax.nn.log_sigmoid(-z), 0.0)
    csum = jnp.cumsum(log_keep, axis=-1)
    weights = jnp.where(before, jnp.exp(jax.nn.log_sigmoid(z) + csum[..., -1:] - csum), 0.0)
    return jnp.einsum('bhts,bshd->bthd', weights, v_ctx.astype(jnp.float32))


def stick_breaking_layer(h, norm_w, w_in, w_out):
    bsz, L, _ = h.shape
    u = rmsnorm(h, norm_w)
    q, k, v, g = jnp.split(u @ w_in, 4, axis=-1)
    q = q.reshape(bsz, L, SB_HEADS, SB_HEAD_DIM)
    k = k.reshape(bsz, L, SB_HEADS, SB_HEAD_DIM)
    v = v.reshape(bsz, L, SB_HEADS, SB_HEAD_DIM)
    bounds = [0] + list(range(N_META, L, SB_BLOCK)) + [L]
    outs = [stick_breaking_block(q[:, s:e], k[:, :e], v[:, :e], s)
            for s, e in zip(bounds[:-1], bounds[1:])]
    o = jnp.concatenate(outs, axis=1).reshape(bsz, L, SB_WIDTH)
    o = (o * jax.nn.silu(g.astype(jnp.float32))).astype(h.dtype)
    return h + o @ w_out


def setup_inputs(seed: int = 0) -> dict:
    key = jax.random.key(seed)
    ks = jax.random.split(key, 24)
    f32 = jnp.float32

    def nrm(k, shape, fan_in):
        return jax.random.normal(k, shape, f32) * (fan_in ** -0.5)

    def gain(k, shape):
        return 1.0 + 0.05 * jax.random.normal(k, shape, f32)

    def bias(k, shape, s=0.05):
        return s * jax.random.normal(k, shape, f32)

    x = jax.random.normal(ks[0], (BATCH, SEQ, D_MODEL), f32)
    meta = jax.random.normal(ks[1], (N_META, D_MODEL), f32)
    even_norm = gain(ks[2], (N_EVEN, D_MODEL))
    even_w_in = nrm(ks[3], (N_EVEN, D_MODEL, EVEN_IN), D_MODEL)
    lru_conv_w = nrm(ks[4], (N_EVEN, CONV_W, LRU_WIDTH), CONV_W)
    lru_conv_b = bias(ks[5], (N_EVEN, LRU_WIDTH))
    lru_w_a = nrm(ks[6], (N_EVEN, LRU_BLOCKS, LRU_BLOCK, LRU_BLOCK), LRU_BLOCK)
    lru_b_a = bias(ks[7], (N_EVEN, LRU_WIDTH), 0.1)
    lru_w_x = nrm(ks[8], (N_EVEN, LRU_BLOCKS, LRU_BLOCK, LRU_BLOCK), LRU_BLOCK)
    lru_b_x = bias(ks[9], (N_EVEN, LRU_WIDTH), 0.1)
    a_c = jax.random.uniform(ks[10], (N_EVEN, LRU_WIDTH), f32, minval=0.9, maxval=0.999)
    a0 = a_c ** (1.0 / RG_LRU_C)
    lru_lambda = jnp.log(a0) - jnp.log1p(-a0)
    ssd_conv_w = nrm(ks[11], (N_EVEN, CONV_W, SSD_CONV_DIM), CONV_W)
    ssd_conv_b = bias(ks[12], (N_EVEN, SSD_CONV_DIM))
    dt0 = jnp.exp(jax.random.uniform(ks[13], (N_EVEN, SSD_HEADS), f32,
                                     minval=float(np.log(1e-3)), maxval=float(np.log(1e-1))))
    ssd_dt_bias = dt0 + jnp.log(-jnp.expm1(-dt0))
    ssd_a_log = jnp.log(jax.random.uniform(ks[14], (N_EVEN, SSD_HEADS), f32, minval=1.0, maxval=16.0))
    ssd_d = gain(ks[15], (N_EVEN, SSD_HEADS))
    ssd_norm = gain(ks[16], (N_EVEN, SSD_WIDTH))
    even_w_out = nrm(ks[17], (N_EVEN, EVEN_MIX, D_MODEL), EVEN_MIX)
    odd_norm = gain(ks[18], (N_ODD, D_MODEL))
    odd_w_in = nrm(ks[19], (N_ODD, D_MODEL, ODD_IN), D_MODEL)
    odd_w_out = nrm(ks[20], (N_ODD, SB_WIDTH, D_MODEL), SB_WIDTH)
    final_norm = gain(ks[21], (D_MODEL,))
    return {'x': x, 'meta': meta, 'even_norm': even_norm, 'even_w_in': even_w_in,
            'lru_conv_w': lru_conv_w, 'lru_conv_b': lru_conv_b,
            'lru_w_a': lru_w_a, 'lru_b_a': lru_b_a, 'lru_w_x': lru_w_x, 'lru_b_x': lru_b_x,
            'lru_lambda': lru_lambda, 'ssd_conv_w': ssd_conv_w, 'ssd_conv_b': ssd_conv_b,
            'ssd_dt_bias': ssd_dt_bias, 'ssd_a_log': ssd_a_log, 'ssd_d': ssd_d,
            'ssd_norm': ssd_norm, 'even_w_out': even_w_out, 'odd_norm': odd_norm,
            'odd_w_in': odd_w_in, 'odd_w_out': odd_w_out, 'final_norm': final_norm}


def reference(x, meta, even_norm, even_w_in, lru_conv_w, lru_conv_b, lru_w_a, lru_b_a,
              lru_w_x, lru_b_x, lru_lambda, ssd_conv_w, ssd_conv_b, ssd_dt_bias,
              ssd_a_log, ssd_d, ssd_norm, even_w_out, odd_norm, odd_w_in, odd_w_out,
              final_norm):
    bsz = x.shape[0]
    meta_b = jnp.broadcast_to(meta[None].astype(x.dtype), (bsz, N_META, D_MODEL))
    h = jnp.concatenate([meta_b, x], axis=1)
    for layer in range(DEPTH):
        j = layer // 2
        if layer % 2 == 0:
            h = rglru_ssd_layer(h, even_norm[j], even_w_in[j], lru_conv_w[j], lru_conv_b[j],
                                lru_w_a[j], lru_b_a[j], lru_w_x[j], lru_b_x[j], lru_lambda[j],
                                ssd_conv_w[j], ssd_conv_b[j], ssd_dt_bias[j], ssd_a_log[j],
                                ssd_d[j], ssd_norm[j], even_w_out[j])
        else:
            h = stick_breaking_layer(h, odd_norm[j], odd_w_in[j], odd_w_out[j])
    return rmsnorm(h, final_norm)[:, N_META:].astype(x.dtype)
```

```python
import functools

import jax
import jax.numpy as jnp
from jax import lax
from jax.experimental import pallas as pl
from jax.experimental.pallas import tpu as pltpu

F32 = jnp.float32
BF16 = jnp.bfloat16

D_MODEL = 1024
N_META = 16
EPS = 1e-6
CONV_W = 4
LRU_WIDTH = 1024
LRU_BLOCKS = 4
LRU_BLOCK = LRU_WIDTH // LRU_BLOCKS
RG_LRU_C = 8.0
SSD_WIDTH = 1024
SSD_HEAD_DIM = 64
SSD_HEADS = 16
SSD_GROUPS = 2
SSD_HPG = SSD_HEADS // SSD_GROUPS
SSD_STATE = 128
SSD_CONV_DIM = SSD_WIDTH + 2 * SSD_GROUPS * SSD_STATE
SB_HEADS = 16
SB_HEAD_DIM = 64
LANES = 128
HEADS_PER_LANE_TILE = LANES // SB_HEAD_DIM
CONV_HALO = 8
F32_EXP_ZERO_BELOW = -104.0
VMEM_LIMIT = 56 * 1024 * 1024

_NT = (((1,), (1,)), ((), ()))
_TN = (((0,), (0,)), ((), ()))


def _params(n_grid):
    return pltpu.CompilerParams(dimension_semantics=("arbitrary",) * n_grid,
                                vmem_limit_bytes=VMEM_LIMIT)


def _const_spec(shape):
    zeros = (0,) * len(shape)
    return pl.BlockSpec(shape, lambda *_: zeros)


def _sigmoid(x):
    return 1.0 / (1.0 + jnp.exp(-x))


def _softplus(x):
    return jnp.maximum(x, 0.0) + jnp.log(1.0 + jnp.exp(-jnp.abs(x)))


def _rmsnorm(x, w):
    ms = jnp.mean(x * x, axis=-1, keepdims=True)
    return (x * lax.rsqrt(ms + EPS)) * w


def _even_in_kernel(chunk, x_ref, nw_ref, w_ref, wdt_ref, wdtt_ref,
                    lx_ref, lg_ref, z_ref, xbc_ref, dt_ref, dtt_ref):
    u = _rmsnorm(x_ref[...], nw_ref[...]).astype(BF16)
    c0, c1, c2, c3 = LRU_WIDTH, 2 * LRU_WIDTH, 2 * LRU_WIDTH + SSD_WIDTH, 2 * LRU_WIDTH + SSD_WIDTH + SSD_CONV_DIM
    lx_ref[...] = jnp.dot(u, w_ref[:, 0:c0], preferred_element_type=F32)
    lg_ref[...] = jnp.dot(u, w_ref[:, c0:c1], preferred_element_type=F32)
    z_ref[...] = jnp.dot(u, w_ref[:, c1:c2], preferred_element_type=F32)
    xbc_ref[...] = jnp.dot(u, w_ref[:, c2:c3], preferred_element_type=F32)
    dt_ref[...] = jnp.dot(u, wdt_ref[...], preferred_element_type=F32)
    for c in range(x_ref.shape[0] // chunk):
        dtt_ref[c] = lax.dot_general(wdtt_ref[...], u[c * chunk:(c + 1) * chunk, :], _NT,
                                     preferred_element_type=F32)


def _even_in(x2d, nw, w_main, w_dt, w_dtt, tile, chunk):
    rows = x2d.shape[0]
    n_main = w_main.shape[1]
    grid = (rows // tile,)
    row_spec = lambda c: pl.BlockSpec((tile, c), lambda i: (i, 0))
    return pl.pallas_call(
        functools.partial(_even_in_kernel, chunk),
        grid=grid,
        in_specs=[row_spec(D_MODEL), _const_spec((1, D_MODEL)), _const_spec((D_MODEL, n_main)),
                  _const_spec((D_MODEL, LANES)), _const_spec((SSD_HEADS, D_MODEL))],
        out_specs=[row_spec(LRU_WIDTH), row_spec(LRU_WIDTH), row_spec(SSD_WIDTH), row_spec(SSD_CONV_DIM),
                   row_spec(LANES),
                   pl.BlockSpec((tile // chunk, SSD_HEADS, chunk), lambda i: (i, 0, 0))],
        out_shape=[jax.ShapeDtypeStruct((rows, LRU_WIDTH), F32), jax.ShapeDtypeStruct((rows, LRU_WIDTH), F32),
                   jax.ShapeDtypeStruct((rows, SSD_WIDTH), F32), jax.ShapeDtypeStruct((rows, SSD_CONV_DIM), F32),
                   jax.ShapeDtypeStruct((rows, LANES), F32),
                   jax.ShapeDtypeStruct((rows // chunk, SSD_HEADS, chunk), F32)],
        compiler_params=_params(1),
        name="even_in",
    )(x2d, nw, w_main, w_dt, w_dtt)


def _conv_from_buf(buf, t, cw, cb):
    acc = cb + cw[CONV_W - 1:CONV_W, :] * buf[CONV_HALO:CONV_HALO + t, :]
    for back in range(1, CONV_W):
        tap = CONV_W - 1 - back
        acc = acc + cw[tap:tap + 1, :] * buf[CONV_HALO - back:CONV_HALO - back + t, :]
    return acc


def _lru_kernel(t, lx_ref, lg_ref, tail_ref, h0_ref, cw_ref, cb_ref, wa_ref, ba_ref, wx_ref, bx_ref, lam_ref,
                ya_ref, tail_out_ref, h_out_ref, xbuf, abuf, bbuf, hbuf, hcar):
    i = pl.program_id(1)

    @pl.when(i == 0)
    def _():
        xbuf[0:CONV_HALO, :] = tail_ref[0]
        hcar[...] = h0_ref[0]

    xbuf[CONV_HALO:CONV_HALO + t, :] = lx_ref[0]
    lx = _conv_from_buf(xbuf, t, cw_ref[...], cb_ref[...])
    lxb = lx.astype(BF16)
    for g in range(LRU_BLOCKS):
        sl = slice(g * LRU_BLOCK, (g + 1) * LRU_BLOCK)
        xg = lxb[:, sl]
        r = _sigmoid(jnp.dot(xg, wa_ref[g], preferred_element_type=F32) + ba_ref[:, sl])
        ig = _sigmoid(jnp.dot(xg, wx_ref[g], preferred_element_type=F32) + bx_ref[:, sl])
        log_a = (-RG_LRU_C * r) * _softplus(-lam_ref[:, sl])
        a = jnp.exp(log_a)
        mult = jnp.sqrt(1.0 - a * a)
        abuf[:, sl] = a
        bbuf[:, sl] = mult * ig * lx[:, sl]

    def step(row, h):
        h = abuf[pl.ds(row, 1), :] * h + bbuf[pl.ds(row, 1), :]
        hbuf[pl.ds(row, 1), :] = h
        return h

    h = lax.fori_loop(0, t, step, hcar[...], unroll=8)
    hcar[...] = h
    lg = lg_ref[0]
    ya_ref[0] = (hbuf[...] * (lg * _sigmoid(lg))).astype(BF16)
    xbuf[0:CONV_HALO, :] = xbuf[t:t + CONV_HALO, :]

    @pl.when(i == pl.num_programs(1) - 1)
    def _():
        tail_out_ref[0] = xbuf[t:t + CONV_HALO, :]
        h_out_ref[0] = h


def _lru(lx, lg, tail, h0, cw, cb, wa, ba, wx, bx, lam, tile):
    bn, seq, _ = lx.shape
    grid = (bn, seq // tile)
    tile_spec = pl.BlockSpec((1, tile, LRU_WIDTH), lambda b, i: (b, i, 0))
    per_b = lambda r: pl.BlockSpec((1, r, LRU_WIDTH), lambda b, i: (b, 0, 0))
    shared = lambda r: pl.BlockSpec((1, r, LRU_WIDTH), lambda b, i: (0, 0, 0))
    return pl.pallas_call(
        functools.partial(_lru_kernel, tile),
        grid=grid,
        in_specs=[tile_spec, tile_spec, shared(CONV_HALO), shared(1),
                  _const_spec((CONV_W, LRU_WIDTH)), _const_spec((1, LRU_WIDTH)),
                  _const_spec((LRU_BLOCKS, LRU_BLOCK, LRU_BLOCK)), _const_spec((1, LRU_WIDTH)),
                  _const_spec((LRU_BLOCKS, LRU_BLOCK, LRU_BLOCK)), _const_spec((1, LRU_WIDTH)),
                  _const_spec((1, LRU_WIDTH))],
        out_specs=[tile_spec, per_b(CONV_HALO), per_b(1)],
        out_shape=[jax.ShapeDtypeStruct((bn, seq, LRU_WIDTH), BF16),
                   jax.ShapeDtypeStruct((bn, CONV_HALO, LRU_WIDTH), F32),
                   jax.ShapeDtypeStruct((bn, 1, LRU_WIDTH), F32)],
        scratch_shapes=[pltpu.VMEM((tile + CONV_HALO, LRU_WIDTH), F32), pltpu.VMEM((tile, LRU_WIDTH), F32),
                        pltpu.VMEM((tile, LRU_WIDTH), F32), pltpu.VMEM((tile, LRU_WIDTH), F32),
                        pltpu.VMEM((1, LRU_WIDTH), F32)],
        compiler_params=_params(2),
        name="rg_lru",
    )(lx, lg, tail, h0, cw, cb, wa, ba, wx, bx, lam)


def _ssd_kernel(t, q, xbc_ref, z_ref, dt_ref, dtt_ref, tail_ref, s0_ref, cw_ref, cb_ref, dtb_ref, dtbt_ref,
                alog_ref, alogt_ref, dexp_ref, nrm_ref,
                yb_ref, tail_out_ref, s_out_ref, xbuf, cbuf, ybuf, sbuf):
    i = pl.program_id(1)

    @pl.when(i == 0)
    def _():
        xbuf[0:CONV_HALO, :] = tail_ref[0]
        sbuf[...] = s0_ref[0]

    xbuf[CONV_HALO:CONV_HALO + t, :] = xbc_ref[0]
    conv = _conv_from_buf(xbuf, t, cw_ref[...], cb_ref[...])
    cbuf[...] = conv * _sigmoid(conv)

    a_row = -jnp.exp(alog_ref[...])
    a_col = -jnp.exp(alogt_ref[...])
    rr = lax.broadcasted_iota(jnp.int32, (q, q), 0)
    cc = lax.broadcasted_iota(jnp.int32, (q, q), 1)
    causal = cc <= rr
    tril = causal.astype(F32)

    def chunk(c, carry):
        r0 = pl.multiple_of(c * q, q)
        rows = pl.ds(r0, q)
        dt = _softplus(dt_ref[0, rows, :] + dtb_ref[...])
        dtt = _softplus(dtt_ref[c] + dtbt_ref[...])
        acum = jnp.dot(tril, dt * a_row, precision=lax.Precision.HIGHEST,
                       preferred_element_type=F32)
        acumt = lax.dot_general(dtt * a_col, tril, _NT, precision=lax.Precision.HIGHEST,
                                preferred_element_type=F32)
        tot = acum[q - 1:q, :]
        w_state = jnp.exp(tot - acum) * dt
        e_acum = jnp.exp(acum)
        e_tot = jnp.exp(tot)
        for g in range(SSD_GROUPS):
            b_g = cbuf[rows, SSD_WIDTH + g * SSD_STATE:SSD_WIDTH + (g + 1) * SSD_STATE]
            c_g = cbuf[rows, SSD_WIDTH + (SSD_GROUPS + g) * SSD_STATE:
                       SSD_WIDTH + (SSD_GROUPS + g + 1) * SSD_STATE].astype(BF16)
            cb_g = lax.dot_general(c_g, b_g.astype(BF16), _NT, preferred_element_type=F32)
            for e in range(SSD_HPG):
                hh = g * SSD_HPG + e
                cols = slice(hh * SSD_HEAD_DIM, (hh + 1) * SSD_HEAD_DIM)
                xs = cbuf[rows, cols].astype(BF16)
                seg = acum[:, hh:hh + 1] - acumt[hh:hh + 1, :]
                decay = jnp.where(causal, jnp.exp(seg), 0.0)
                m = (cb_g * decay * dtt[hh:hh + 1, :]).astype(BF16)
                state = sbuf[hh]
                y = jnp.dot(m, xs, preferred_element_type=F32)
                y = y + jnp.dot(c_g, state.astype(BF16), preferred_element_type=F32) * e_acum[:, hh:hh + 1]
                ybuf[rows, cols] = y
                bw = (b_g * w_state[:, hh:hh + 1]).astype(BF16)
                sbuf[hh] = e_tot[:, hh:hh + 1] * state + lax.dot_general(
                    bw, xs, _TN, preferred_element_type=F32)
        return carry

    lax.fori_loop(0, t // q, chunk, 0)

    y = ybuf[...] + cbuf[:, 0:SSD_WIDTH] * dexp_ref[...]
    zz = z_ref[0]
    gt = y * (zz * _sigmoid(zz))
    gw = SSD_WIDTH // SSD_GROUPS
    for g in range(SSD_GROUPS):
        gg = gt[:, g * gw:(g + 1) * gw]
        ms = jnp.mean(gg * gg, axis=-1, keepdims=True)
        yb_ref[0, :, g * gw:(g + 1) * gw] = (gg * lax.rsqrt(ms + EPS) * nrm_ref[:, g * gw:(g + 1) * gw]).astype(BF16)
    xbuf[0:CONV_HALO, :] = xbuf[t:t + CONV_HALO, :]

    @pl.when(i == pl.num_programs(1) - 1)
    def _():
        tail_out_ref[0] = xbuf[t:t + CONV_HALO, :]
        s_out_ref[0] = sbuf[...]


def _ssd(xbc, z, dt, dtt, tail, s0, cw, cb, dtb, dtbt, alog, alogt, dexp, nrm, tile, chunk):
    bn, seq, _ = xbc.shape
    nt = seq // tile
    grid = (bn, nt)
    tile_spec = lambda c: pl.BlockSpec((1, tile, c), lambda b, i: (b, i, 0))
    state_shape = (SSD_HEADS, SSD_STATE, SSD_HEAD_DIM)
    return pl.pallas_call(
        functools.partial(_ssd_kernel, tile, chunk),
        grid=grid,
        in_specs=[tile_spec(SSD_CONV_DIM), tile_spec(SSD_WIDTH), tile_spec(LANES),
                  pl.BlockSpec((tile // chunk, SSD_HEADS, chunk), lambda b, i: (b * nt + i, 0, 0)),
                  pl.BlockSpec((1, CONV_HALO, SSD_CONV_DIM), lambda b, i: (0, 0, 0)),
                  pl.BlockSpec((1,) + state_shape, lambda b, i: (0, 0, 0, 0)),
                  _const_spec((CONV_W, SSD_CONV_DIM)), _const_spec((1, SSD_CONV_DIM)),
                  _const_spec((1, LANES)), _const_spec((SSD_HEADS, 1)),
                  _const_spec((1, LANES)), _const_spec((SSD_HEADS, 1)),
                  _const_spec((1, SSD_WIDTH)), _const_spec((1, SSD_WIDTH))],
        out_specs=[tile_spec(SSD_WIDTH),
                   pl.BlockSpec((1, CONV_HALO, SSD_CONV_DIM), lambda b, i: (b, 0, 0)),
                   pl.BlockSpec((1,) + state_shape, lambda b, i: (b, 0, 0, 0))],
        out_shape=[jax.ShapeDtypeStruct((bn, seq, SSD_WIDTH), BF16),
                   jax.ShapeDtypeStruct((bn, CONV_HALO, SSD_CONV_DIM), F32),
                   jax.ShapeDtypeStruct((bn,) + state_shape, F32)],
        scratch_shapes=[pltpu.VMEM((tile + CONV_HALO, SSD_CONV_DIM), F32), pltpu.VMEM((tile, SSD_CONV_DIM), F32),
                        pltpu.VMEM((tile, SSD_WIDTH), F32), pltpu.VMEM(state_shape, F32)],
        compiler_params=_params(2),
        name="ssd",
    )(xbc, z, dt, dtt, tail, s0, cw, cb, dtb, dtbt, alog, alogt, dexp, nrm)


def _mid_kernel(x_ref, ya_ref, yb_ref, wo_ref, nw_ref, wi_ref, h1_ref, q_ref, k_ref, v_ref, g_ref):
    h1 = (x_ref[...] + jnp.dot(ya_ref[...], wo_ref[0:LRU_WIDTH, :], preferred_element_type=F32)
          + jnp.dot(yb_ref[...], wo_ref[LRU_WIDTH:LRU_WIDTH + SSD_WIDTH, :], preferred_element_type=F32))
    h1_ref[...] = h1
    u = _rmsnorm(h1, nw_ref[...]).astype(BF16)
    w = D_MODEL
    q_ref[...] = (jnp.dot(u, wi_ref[:, 0:w], preferred_element_type=F32) * (SB_HEAD_DIM ** -0.5)).astype(BF16)
    k_ref[...] = jnp.dot(u, wi_ref[:, w:2 * w], preferred_element_type=F32).astype(BF16)
    v_ref[...] = jnp.dot(u, wi_ref[:, 2 * w:3 * w], preferred_element_type=F32).astype(BF16)
    g_ref[...] = jnp.dot(u, wi_ref[:, 3 * w:4 * w], preferred_element_type=F32)


def _mid(x2d, ya, yb, wo, nw, wi, tile):
    rows = x2d.shape[0]
    row_spec = pl.BlockSpec((tile, D_MODEL), lambda i: (i, 0))
    return pl.pallas_call(
        _mid_kernel,
        grid=(rows // tile,),
        in_specs=[row_spec, row_spec, row_spec, _const_spec((LRU_WIDTH + SSD_WIDTH, D_MODEL)),
                  _const_spec((1, D_MODEL)), _const_spec((D_MODEL, 4 * D_MODEL))],
        out_specs=[row_spec] * 5,
        out_shape=[jax.ShapeDtypeStruct((rows, D_MODEL), F32)] + [jax.ShapeDtypeStruct((rows, D_MODEL), BF16)] * 3
                  + [jax.ShapeDtypeStruct((rows, D_MODEL), F32)],
        compiler_params=_params(1),
        name="even_out_odd_in",
    )(x2d, ya, yb, wo, nw, wi)


def _sb_block(qm, kb, vb, r, before):
    tk = kb.shape[0]
    z = lax.dot_general(qm, kb, _NT, preferred_element_type=F32)
    sp = _softplus(z)
    lk = -sp
    if before is not None:
        lk = jnp.where(before, lk, 0.0)
    jj = lax.broadcasted_iota(jnp.int32, (tk, tk), 0)
    ss = lax.broadcasted_iota(jnp.int32, (tk, tk), 1)
    later = (jj > ss).astype(BF16)
    hi = lk.astype(BF16)
    lo = (lk - hi.astype(F32)).astype(BF16)
    csum = (jnp.dot(hi, later, preferred_element_type=F32)
            + jnp.dot(lo, later, preferred_element_type=F32))
    wgt = jnp.exp((z - sp) + csum + r)
    if before is not None:
        wgt = jnp.where(before, wgt, 0.0)
    out = jnp.dot(wgt.astype(BF16), vb, preferred_element_type=F32)
    return out, r + jnp.sum(lk, axis=-1, keepdims=True)


def _attn_kernel(tq, q_ref, k_ref, v_ref, km_ref, vm_ref, g_ref, o_ref, acc_ref, r_ref):
    i = pl.program_id(2)
    q = q_ref[0]
    lane = lax.broadcasted_iota(jnp.int32, (tq, LANES), 1)
    head_lanes = [(lane >= h * SB_HEAD_DIM) & (lane < (h + 1) * SB_HEAD_DIM) for h in range(HEADS_PER_LANE_TILE)]
    qh = [jnp.where(m, q, jnp.zeros_like(q)) for m in head_lanes]

    r0 = pl.multiple_of(i * tq, tq)
    kb = k_ref[0, pl.ds(r0, tq), :]
    vb = v_ref[0, pl.ds(r0, tq), :]
    before = lax.broadcasted_iota(jnp.int32, (tq, tq), 1) < lax.broadcasted_iota(jnp.int32, (tq, tq), 0)
    rmax = None
    for h in range(HEADS_PER_LANE_TILE):
        out, r = _sb_block(qh[h], kb, vb, jnp.zeros((tq, 1), F32), before)
        acc_ref[h] = out
        r_ref[h] = r
        m = jnp.max(r)
        rmax = m if rmax is None else jnp.maximum(rmax, m)

    def cond(c):
        j, rm = c
        return (j >= 0) & (rm > F32_EXP_ZERO_BELOW)

    def body(c):
        j, _ = c
        s0 = pl.multiple_of(j * tq, tq)
        kj = k_ref[0, pl.ds(s0, tq), :]
        vj = v_ref[0, pl.ds(s0, tq), :]
        rm = None
        for h in range(HEADS_PER_LANE_TILE):
            out, r = _sb_block(qh[h], kj, vj, r_ref[h], None)
            acc_ref[h] += out
            r_ref[h] = r
            m = jnp.max(r)
            rm = m if rm is None else jnp.maximum(rm, m)
        return j - 1, rm

    _, rmax = lax.while_loop(cond, body, (i - 1, rmax))

    @pl.when(rmax > F32_EXP_ZERO_BELOW)
    def _():
        for h in range(HEADS_PER_LANE_TILE):
            out, _ = _sb_block(qh[h], km_ref[0], vm_ref[0], r_ref[h], None)
            acc_ref[h] += out

    o = acc_ref[0]
    for h in range(1, HEADS_PER_LANE_TILE):
        o = jnp.where(head_lanes[h], acc_ref[h], o)
    g = g_ref[0]
    o_ref[0] = (o * (g * _sigmoid(g))).astype(BF16)


def _attn(q, k, v, km, vm, g, tq):
    bn, seq, width = q.shape
    n_blk = width // LANES
    grid = (bn, n_blk, seq // tq)
    q_spec = pl.BlockSpec((1, tq, LANES), lambda b, p, i: (b, i, p))
    kv_spec = pl.BlockSpec((1, seq, LANES), lambda b, p, i: (b, 0, p))
    m_spec = pl.BlockSpec((1, N_META, LANES), lambda b, p, i: (0, 0, p))
    return pl.pallas_call(
        functools.partial(_attn_kernel, tq),
        grid=grid,
        in_specs=[q_spec, kv_spec, kv_spec, m_spec, m_spec, q_spec],
        out_specs=q_spec,
        out_shape=jax.ShapeDtypeStruct((bn, seq, width), BF16),
        scratch_shapes=[pltpu.VMEM((HEADS_PER_LANE_TILE, tq, LANES), F32),
                        pltpu.VMEM((HEADS_PER_LANE_TILE, tq, 1), F32)],
        compiler_params=_params(3),
        name="stick_breaking",
    )(q, k, v, km, vm, g)


def _final_kernel(h1_ref, o_ref, wo_ref, fw_ref, out_ref):
    h2 = h1_ref[...] + jnp.dot(o_ref[...], wo_ref[...], preferred_element_type=F32)
    out_ref[...] = _rmsnorm(h2, fw_ref[...])


def _final(h1, o, wo, fw, tile):
    rows = h1.shape[0]
    row_spec = pl.BlockSpec((tile, D_MODEL), lambda i: (i, 0))
    return pl.pallas_call(
        _final_kernel,
        grid=(rows // tile,),
        in_specs=[row_spec, row_spec, _const_spec((D_MODEL, D_MODEL)), _const_spec((1, D_MODEL))],
        out_specs=row_spec,
        out_shape=jax.ShapeDtypeStruct((rows, D_MODEL), F32),
        compiler_params=_params(1),
        name="odd_out_final_norm",
    )(h1, o, wo, fw)


def _layer0_and_qkv(x3, carry, p, row_tile, lru_tile, ssd_tile, chunk):
    bn, seq, _ = x3.shape
    rows = bn * seq
    x2d = x3.reshape(rows, D_MODEL)
    lx, lg, z, xbc, dt, dtt = _even_in(x2d, p["even_norm"], p["w_main"], p["w_dt"], p["w_dtt"], row_tile, chunk)
    r3 = lambda a: a.reshape(bn, seq, a.shape[-1])
    ya, lru_tail, lru_h = _lru(r3(lx), r3(lg), carry["lru_tail"], carry["lru_h"], p["lru_conv_w"], p["lru_conv_b"],
                               p["lru_w_a"], p["lru_b_a"], p["lru_w_x"], p["lru_b_x"], p["lru_lambda"], lru_tile)
    yb, ssd_tail, ssd_s = _ssd(r3(xbc), r3(z), r3(dt), dtt, carry["ssd_tail"], carry["ssd_s"],
                               p["ssd_conv_w"], p["ssd_conv_b"], p["dt_bias_row"], p["dt_bias_col"],
                               p["a_log_row"], p["a_log_col"], p["d_expanded"], p["ssd_norm"], ssd_tile, chunk)
    h1, q, k, v, g = _mid(x2d, ya.reshape(rows, -1), yb.reshape(rows, -1), p["even_w_out"], p["odd_norm"],
                          p["odd_w_in"], row_tile)
    new_carry = {"lru_tail": lru_tail, "lru_h": lru_h, "ssd_tail": ssd_tail, "ssd_s": ssd_s}
    return h1, r3(q), r3(k), r3(v), r3(g), new_carry


def kernel(x, meta, even_norm, even_w_in, lru_conv_w, lru_conv_b, lru_w_a, lru_b_a, lru_w_x, lru_b_x, lru_lambda,
           ssd_conv_w, ssd_conv_b, ssd_dt_bias, ssd_a_log, ssd_d, ssd_norm, even_w_out, odd_norm, odd_w_in,
           odd_w_out, final_norm):
    bsz, seq, _ = x.shape
    n_main = 2 * LRU_WIDTH + SSD_WIDTH + SSD_CONV_DIM
    w_in = even_w_in[0]
    w_dt = w_in[:, n_main:]

    def lane_row(vec):
        return jnp.pad(vec.astype(F32), (0, LANES - vec.shape[0])).reshape(1, LANES)

    p = {
        "even_norm": even_norm[0].reshape(1, -1),
        "w_main": w_in[:, :n_main].astype(BF16),
        "w_dt": jnp.pad(w_dt, ((0, 0), (0, LANES - SSD_HEADS))).astype(BF16),
        "w_dtt": w_dt.T.astype(BF16),
        "lru_conv_w": lru_conv_w[0], "lru_conv_b": lru_conv_b[0].reshape(1, -1),
        "lru_w_a": lru_w_a[0].astype(BF16), "lru_b_a": lru_b_a[0].reshape(1, -1),
        "lru_w_x": lru_w_x[0].astype(BF16), "lru_b_x": lru_b_x[0].reshape(1, -1),
        "lru_lambda": lru_lambda[0].reshape(1, -1),
        "ssd_conv_w": ssd_conv_w[0], "ssd_conv_b": ssd_conv_b[0].reshape(1, -1),
        "dt_bias_row": lane_row(ssd_dt_bias[0]), "dt_bias_col": ssd_dt_bias[0].reshape(-1, 1),
        "a_log_row": lane_row(ssd_a_log[0]), "a_log_col": ssd_a_log[0].reshape(-1, 1),
        "d_expanded": jnp.repeat(ssd_d[0], SSD_HEAD_DIM).reshape(1, -1),
        "ssd_norm": ssd_norm[0].reshape(1, -1),
        "even_w_out": even_w_out[0].astype(BF16),
        "odd_norm": odd_norm[0].reshape(1, -1),
        "odd_w_in": odd_w_in[0].astype(BF16),
    }
    zero_carry = {
        "lru_tail": jnp.zeros((1, CONV_HALO, LRU_WIDTH), F32), "lru_h": jnp.zeros((1, 1, LRU_WIDTH), F32),
        "ssd_tail": jnp.zeros((1, CONV_HALO, SSD_CONV_DIM), F32),
        "ssd_s": jnp.zeros((1, SSD_HEADS, SSD_STATE, SSD_HEAD_DIM), F32),
    }
    _, _, km, vm, _, carry = _layer0_and_qkv(meta.astype(x.dtype)[None], zero_carry, p,
                                             row_tile=N_META, lru_tile=N_META, ssd_tile=N_META, chunk=N_META)
    row_tile = min(512, seq)
    h1, q, k, v, g, _ = _layer0_and_qkv(x, carry, p, row_tile=row_tile, lru_tile=min(256, seq),
                                        ssd_tile=min(512, seq), chunk=64)
    o = _attn(q, k, v, km, vm, g, tq=min(256, seq))
    out = _final(h1, o.reshape(bsz * seq, D_MODEL), odd_w_out[0].astype(BF16), final_norm.reshape(1, -1), row_tile)
    return out.reshape(bsz, seq, D_MODEL)
```

```python
import functools

import jax
import jax.numpy as jnp
from jax import lax
from jax.experimental import pallas as pl
from jax.experimental.pallas import tpu as pltpu

F32 = jnp.float32
BF16 = jnp.bfloat16

D_MODEL = 1024
N_META = 16
EPS = 1e-6
CONV_W = 4
LRU_WIDTH = 1024
LRU_BLOCKS = 4
LRU_BLOCK = LRU_WIDTH // LRU_BLOCKS
RG_LRU_C = 8.0
SSD_WIDTH = 1024
SSD_HEAD_DIM = 64
SSD_HEADS = 16
SSD_GROUPS = 2
SSD_HPG = SSD_HEADS // SSD_GROUPS
SSD_STATE = 128
SSD_CONV_DIM = SSD_WIDTH + 2 * SSD_GROUPS * SSD_STATE
SSD_GROUP_WIDTH = SSD_WIDTH // SSD_GROUPS
SB_HEADS = 16
SB_HEAD_DIM = 64
LANES = 128
HEADS_PER_LANE_TILE = LANES // SB_HEAD_DIM
SSD_PAIRS = SSD_HEADS // HEADS_PER_LANE_TILE
SSD_PAIRS_PER_GROUP = SSD_PAIRS // SSD_GROUPS
CONV_HALO = 8
F32_EXP_ZERO_BELOW = -104.0
LOG2_E = 1.4426950408889634
VMEM_LIMIT = 56 * 1024 * 1024

_NT = (((1,), (1,)), ((), ()))
_TN = (((0,), (0,)), ((), ()))
_EXACT = lax.Precision.HIGHEST


def _params(n_grid):
    return pltpu.CompilerParams(dimension_semantics=("arbitrary",) * n_grid,
                                vmem_limit_bytes=VMEM_LIMIT)


def _const_spec(shape):
    zeros = (0,) * len(shape)
    return pl.BlockSpec(shape, lambda *_: zeros)


def _sigmoid(x):
    return 1.0 / (1.0 + jnp.exp(-x))


def _softplus(x):
    return jnp.maximum(x, 0.0) + jnp.log(1.0 + jnp.exp(-jnp.abs(x)))


def _rmsnorm(x, w):
    ms = jnp.mean(x * x, axis=-1, keepdims=True)
    return (x * lax.rsqrt(ms + EPS)) * w


def _causal_conv(x, prev, cw, cb):
    assert x.shape[0] > CONV_HALO
    acc = cb + cw[CONV_W - 1:CONV_W, :] * x
    row = lax.broadcasted_iota(jnp.int32, (CONV_HALO, x.shape[1]), 0)
    for back in range(1, CONV_W):
        tap = CONV_W - 1 - back
        rolled = pltpu.roll(x, back, axis=0)
        head = jnp.where(row < back, pltpu.roll(prev, back, axis=0), rolled[0:CONV_HALO, :])
        acc = acc + cw[tap:tap + 1, :] * jnp.concatenate([head, rolled[CONV_HALO:, :]], axis=0)
    return acc


def _even_in_kernel(t, chunk, x_ref, ltail_ref, xtail_ref, nw_ref, w_ref, wdt_ref, wdte_ref, wdto_ref,
                    lcw_ref, lcb_ref, xcw_ref, xcb_ref,
                    lxc_ref, lg_ref, z_ref, xc_ref, dt_ref, dtp_ref, ltail_out_ref, xtail_out_ref, lbuf, xbuf):
    i = pl.program_id(1)

    @pl.when(i == 0)
    def _():
        lbuf[...] = ltail_ref[0]
        xbuf[...] = xtail_ref[0]

    u = _rmsnorm(x_ref[0], nw_ref[...]).astype(BF16)
    c0, c1, c2, c3 = LRU_WIDTH, 2 * LRU_WIDTH, 2 * LRU_WIDTH + SSD_WIDTH, 2 * LRU_WIDTH + SSD_WIDTH + SSD_CONV_DIM
    lx = jnp.dot(u, w_ref[:, 0:c0], preferred_element_type=F32)
    lxc_ref[0] = _causal_conv(lx, lbuf[...], lcw_ref[...], lcb_ref[...])
    lg_ref[0] = jnp.dot(u, w_ref[:, c0:c1], preferred_element_type=F32)
    z_ref[0] = jnp.dot(u, w_ref[:, c1:c2], preferred_element_type=F32)
    xbc = jnp.dot(u, w_ref[:, c2:c3], preferred_element_type=F32)
    conv = _causal_conv(xbc, xbuf[...], xcw_ref[...], xcb_ref[...])
    xc_ref[0] = conv * _sigmoid(conv)
    dt_ref[0] = jnp.dot(u, wdt_ref[...], preferred_element_type=F32)
    for c in range(t // chunk):
        uc = u[c * chunk:(c + 1) * chunk, :]
        even = lax.dot_general(wdte_ref[...], uc, _NT, preferred_element_type=F32)
        odd = lax.dot_general(wdto_ref[...], uc, _NT, preferred_element_type=F32)
        dtp_ref[c] = jnp.concatenate([even, odd], axis=1)

    lbuf[...] = lx[t - CONV_HALO:t, :]
    xbuf[...] = xbc[t - CONV_HALO:t, :]

    @pl.when(i == pl.num_programs(1) - 1)
    def _():
        ltail_out_ref[0] = lbuf[...]
        xtail_out_ref[0] = xbuf[...]


def _even_in(x3, ltail, xtail, p, tile, chunk):
    bn, seq, _ = x3.shape
    nt = seq // tile
    n_main = p["w_main"].shape[1]
    tile_spec = lambda c: pl.BlockSpec((1, tile, c), lambda b, i: (b, i, 0))
    shared = lambda c: pl.BlockSpec((1, CONV_HALO, c), lambda b, i: (0, 0, 0))
    per_b = lambda c: pl.BlockSpec((1, CONV_HALO, c), lambda b, i: (b, 0, 0))
    act = lambda c, dt=F32: jax.ShapeDtypeStruct((bn, seq, c), dt)
    return pl.pallas_call(
        functools.partial(_even_in_kernel, tile, chunk),
        grid=(bn, nt),
        in_specs=[tile_spec(D_MODEL), shared(LRU_WIDTH), shared(SSD_CONV_DIM),
                  _const_spec((1, D_MODEL)), _const_spec((D_MODEL, n_main)), _const_spec((D_MODEL, LANES)),
                  _const_spec((SSD_PAIRS, D_MODEL)), _const_spec((SSD_PAIRS, D_MODEL)),
                  _const_spec((CONV_W, LRU_WIDTH)), _const_spec((1, LRU_WIDTH)),
                  _const_spec((CONV_W, SSD_CONV_DIM)), _const_spec((1, SSD_CONV_DIM))],
        out_specs=[tile_spec(LRU_WIDTH), tile_spec(LRU_WIDTH), tile_spec(SSD_WIDTH), tile_spec(SSD_CONV_DIM),
                   tile_spec(LANES),
                   pl.BlockSpec((tile // chunk, SSD_PAIRS, 2 * chunk), lambda b, i: (b * nt + i, 0, 0)),
                   per_b(LRU_WIDTH), per_b(SSD_CONV_DIM)],
        out_shape=[act(LRU_WIDTH), act(LRU_WIDTH), act(SSD_WIDTH), act(SSD_CONV_DIM), act(LANES),
                   jax.ShapeDtypeStruct((bn * seq // chunk, SSD_PAIRS, 2 * chunk), F32),
                   jax.ShapeDtypeStruct((bn, CONV_HALO, LRU_WIDTH), F32),
                   jax.ShapeDtypeStruct((bn, CONV_HALO, SSD_CONV_DIM), F32)],
        scratch_shapes=[pltpu.VMEM((CONV_HALO, LRU_WIDTH), F32), pltpu.VMEM((CONV_HALO, SSD_CONV_DIM), F32)],
        compiler_params=_params(2),
        name="even_in",
    )(x3, ltail, xtail, p["even_norm"], p["w_main"], p["w_dt"], p["w_dt_even"], p["w_dt_odd"],
      p["lru_conv_w"], p["lru_conv_b"], p["ssd_conv_w"], p["ssd_conv_b"])


def _lru_kernel(t, lx_ref, lg_ref, h0_ref, wa_ref, ba_ref, wx_ref, bx_ref, lam_ref,
                ya_ref, h_out_ref, abuf, bbuf, hbuf, hcar):
    i = pl.program_id(1)

    @pl.when(i == 0)
    def _():
        hcar[...] = h0_ref[0]

    lx = lx_ref[0]
    lxb = lx.astype(BF16)
    for g in range(LRU_BLOCKS):
        sl = slice(g * LRU_BLOCK, (g + 1) * LRU_BLOCK)
        xg = lxb[:, sl]
        r = _sigmoid(jnp.dot(xg, wa_ref[g], preferred_element_type=F32) + ba_ref[:, sl])
        ig = _sigmoid(jnp.dot(xg, wx_ref[g], preferred_element_type=F32) + bx_ref[:, sl])
        log_a = (-RG_LRU_C * r) * _softplus(-lam_ref[:, sl])
        a = jnp.exp(log_a)
        mult = jnp.sqrt(1.0 - a * a)
        abuf[:, sl] = a
        bbuf[:, sl] = mult * ig * lx[:, sl]

    def step(row, h):
        h = abuf[pl.ds(row, 1), :] * h + bbuf[pl.ds(row, 1), :]
        hbuf[pl.ds(row, 1), :] = h
        return h

    h = lax.fori_loop(0, t, step, hcar[...], unroll=8)
    hcar[...] = h
    lg = lg_ref[0]
    ya_ref[0] = (hbuf[...] * (lg * _sigmoid(lg))).astype(BF16)

    @pl.when(i == pl.num_programs(1) - 1)
    def _():
        h_out_ref[0] = h


def _lru(lxc, lg, h0, p, tile):
    bn, seq, _ = lxc.shape
    tile_spec = pl.BlockSpec((1, tile, LRU_WIDTH), lambda b, i: (b, i, 0))
    return pl.pallas_call(
        functools.partial(_lru_kernel, tile),
        grid=(bn, seq // tile),
        in_specs=[tile_spec, tile_spec, pl.BlockSpec((1, 1, LRU_WIDTH), lambda b, i: (0, 0, 0)),
                  _const_spec((LRU_BLOCKS, LRU_BLOCK, LRU_BLOCK)), _const_spec((1, LRU_WIDTH)),
                  _const_spec((LRU_BLOCKS, LRU_BLOCK, LRU_BLOCK)), _const_spec((1, LRU_WIDTH)),
                  _const_spec((1, LRU_WIDTH))],
        out_specs=[tile_spec, pl.BlockSpec((1, 1, LRU_WIDTH), lambda b, i: (b, 0, 0))],
        out_shape=[jax.ShapeDtypeStruct((bn, seq, LRU_WIDTH), BF16),
                   jax.ShapeDtypeStruct((bn, 1, LRU_WIDTH), F32)],
        scratch_shapes=[pltpu.VMEM((tile, LRU_WIDTH), F32), pltpu.VMEM((tile, LRU_WIDTH), F32),
                        pltpu.VMEM((tile, LRU_WIDTH), F32), pltpu.VMEM((1, LRU_WIDTH), F32)],
        compiler_params=_params(2),
        name="rg_lru",
    )(lxc, lg, h0, p["lru_w_a"], p["lru_b_a"], p["lru_w_x"], p["lru_b_x"], p["lru_lambda"])


def _ssd_kernel(t, q, xc_ref, z_ref, dt_ref, dtp_ref, s0_ref, dtb_ref, dtbp_ref, alog_ref, alogp_ref,
                dexp_ref, nrm_ref, yb_ref, s_out_ref, ybuf, sbuf):
    i = pl.program_id(1)

    @pl.when(i == 0)
    def _():
        sbuf[...] = s0_ref[0]

    a_row = -jnp.exp(alog_ref[...])
    a_pk = -jnp.exp(alogp_ref[...])
    rr = lax.broadcasted_iota(jnp.int32, (q, q), 0)
    cc = lax.broadcasted_iota(jnp.int32, (q, q), 1)
    tril = (cc <= rr).astype(F32)
    r2 = lax.broadcasted_iota(jnp.int32, (2 * q, 2 * q), 0)
    c2 = lax.broadcasted_iota(jnp.int32, (2 * q, 2 * q), 1)
    bd = ((r2 <= c2) & ((r2 < q) == (c2 < q))).astype(F32)
    lane_x = lax.broadcasted_iota(jnp.int32, (q, LANES), 1)
    first_x = lane_x < SSD_HEAD_DIM
    lane_m = lax.broadcasted_iota(jnp.int32, (q, 2 * q), 1)
    first_m = lane_m < q
    row_m = lax.broadcasted_iota(jnp.int32, (q, 2 * q), 0)
    causal_m = jnp.where(first_m, lane_m, lane_m - q) <= row_m

    def chunk(c, carry):
        rows = pl.ds(pl.multiple_of(c * q, q), q)
        dt = _softplus(dt_ref[0, rows, :] + dtb_ref[...])
        dtp = _softplus(dtp_ref[c] + dtbp_ref[...])
        acum = jnp.dot(tril, dt * a_row, precision=_EXACT, preferred_element_type=F32)
        acum_pk = jnp.dot(dtp * a_pk, bd, precision=_EXACT, preferred_element_type=F32)
        for g in range(SSD_GROUPS):
            b_lo = SSD_WIDTH + g * SSD_STATE
            c_lo = SSD_WIDTH + (SSD_GROUPS + g) * SSD_STATE
            b_g = xc_ref[0, rows, b_lo:b_lo + SSD_STATE].astype(BF16)
            c_g = xc_ref[0, rows, c_lo:c_lo + SSD_STATE].astype(BF16)
            cbb = lax.dot_general(c_g, jnp.concatenate([b_g, b_g], axis=0), _NT,
                                  preferred_element_type=F32)
            y_diag, wx, e_acum, e_tot = [], [], [], []
            for pp in range(SSD_PAIRS_PER_GROUP):
                pr = g * SSD_PAIRS_PER_GROUP + pp
                h0, h1 = 2 * pr, 2 * pr + 1
                acol_x = jnp.where(first_x, acum[:, h0:h0 + 1], acum[:, h1:h1 + 1])
                dt_x = jnp.where(first_x, dt[:, h0:h0 + 1], dt[:, h1:h1 + 1])
                if 2 * q == LANES:
                    acol_m = acol_x
                else:
                    acol_m = jnp.where(first_m, acum[:, h0:h0 + 1], acum[:, h1:h1 + 1])
                xdt = xc_ref[0, rows, pr * LANES:(pr + 1) * LANES] * dt_x
                xdt_b = xdt.astype(BF16)
                decay = jnp.where(causal_m, jnp.exp(acol_m - acum_pk[pr:pr + 1, :]), 0.0)
                m = (cbb * decay).astype(BF16)
                zeros = jnp.zeros_like(xdt_b)
                rhs = jnp.concatenate([jnp.where(first_x, xdt_b, zeros), jnp.where(first_x, zeros, xdt_b)],
                                      axis=0)
                y_diag.append(jnp.dot(m, rhs, preferred_element_type=F32))
                tot_x = acol_x[q - 1:q, :]
                wx.append((jnp.exp(tot_x - acol_x) * xdt).astype(BF16))
                e_acum.append(jnp.exp(acol_x))
                e_tot.append(jnp.exp(tot_x))
            state = sbuf[g]
            y_off = jnp.dot(c_g, state.astype(BF16), preferred_element_type=F32)
            cols = slice(g * SSD_GROUP_WIDTH, (g + 1) * SSD_GROUP_WIDTH)
            ybuf[rows, cols] = jnp.concatenate(y_diag, axis=1) + y_off * jnp.concatenate(e_acum, axis=1)
            sbuf[g] = jnp.concatenate(e_tot, axis=1) * state + lax.dot_general(
                b_g, jnp.concatenate(wx, axis=1), _TN, preferred_element_type=F32)
        return carry

    lax.fori_loop(0, t // q, chunk, 0, unroll=min(2, t // q))

    y = ybuf[...] + xc_ref[0, :, 0:SSD_WIDTH] * dexp_ref[...]
    zz = z_ref[0]
    gt = y * (zz * _sigmoid(zz))
    for g in range(SSD_GROUPS):
        cols = slice(g * SSD_GROUP_WIDTH, (g + 1) * SSD_GROUP_WIDTH)
        gg = gt[:, cols]
        ms = jnp.mean(gg * gg, axis=-1, keepdims=True)
        yb_ref[0, :, cols] = (gg * lax.rsqrt(ms + EPS) * nrm_ref[:, cols]).astype(BF16)

    @pl.when(i == pl.num_programs(1) - 1)
    def _():
        s_out_ref[0] = sbuf[...]


def _ssd(xc, z, dt, dtp, s0, p, dtbp, alogp, tile, chunk):
    bn, seq, _ = xc.shape
    nt = seq // tile
    tile_spec = lambda c: pl.BlockSpec((1, tile, c), lambda b, i: (b, i, 0))
    state_shape = (SSD_GROUPS, SSD_STATE, SSD_GROUP_WIDTH)
    return pl.pallas_call(
        functools.partial(_ssd_kernel, tile, chunk),
        grid=(bn, nt),
        in_specs=[tile_spec(SSD_CONV_DIM), tile_spec(SSD_WIDTH), tile_spec(LANES),
                  pl.BlockSpec((tile // chunk, SSD_PAIRS, 2 * chunk), lambda b, i: (b * nt + i, 0, 0)),
                  pl.BlockSpec((1,) + state_shape, lambda b, i: (0, 0, 0, 0)),
                  _const_spec((1, LANES)), _const_spec((SSD_PAIRS, 2 * chunk)),
                  _const_spec((1, LANES)), _const_spec((SSD_PAIRS, 2 * chunk)),
                  _const_spec((1, SSD_WIDTH)), _const_spec((1, SSD_WIDTH))],
        out_specs=[tile_spec(SSD_WIDTH), pl.BlockSpec((1,) + state_shape, lambda b, i: (b, 0, 0, 0))],
        out_shape=[jax.ShapeDtypeStruct((bn, seq, SSD_WIDTH), BF16),
                   jax.ShapeDtypeStruct((bn,) + state_shape, F32)],
        scratch_shapes=[pltpu.VMEM((tile, SSD_WIDTH), F32), pltpu.VMEM(state_shape, F32)],
        compiler_params=_params(2),
        name="ssd",
    )(xc, z, dt, dtp, s0, p["dt_bias_row"], dtbp, p["a_log_row"], alogp, p["d_expanded"], p["ssd_norm"])


def _mid_kernel(x_ref, ya_ref, yb_ref, wo_ref, nw_ref, wi_ref, *rest):
    h1_ref, q_ref, k_ref, v_ref, g_ref = rest[-5:]
    h1 = (x_ref[0] + jnp.dot(ya_ref[0], wo_ref[0:LRU_WIDTH, :], preferred_element_type=F32)
          + jnp.dot(yb_ref[0], wo_ref[LRU_WIDTH:LRU_WIDTH + SSD_WIDTH, :], preferred_element_type=F32))
    h1_ref[0] = h1
    u = _rmsnorm(h1, nw_ref[...]).astype(BF16)
    w = D_MODEL
    q_ref[0] = (jnp.dot(u, wi_ref[:, 0:w], preferred_element_type=F32) * (SB_HEAD_DIM ** -0.5)).astype(BF16)
    k_ref[0] = jnp.dot(u, wi_ref[:, w:2 * w], preferred_element_type=F32).astype(BF16)
    v_ref[0] = jnp.dot(u, wi_ref[:, 2 * w:3 * w], preferred_element_type=F32).astype(BF16)
    g_ref[0] = jnp.dot(u, wi_ref[:, 3 * w:4 * w], preferred_element_type=F32)


def _mid(x3, ya, yb, p, tile, kv_init=None):
    bn, seq, _ = x3.shape
    tile_spec = pl.BlockSpec((1, tile, D_MODEL), lambda b, i: (b, i, 0))
    act = lambda dt: jax.ShapeDtypeStruct((bn, seq, D_MODEL), dt)
    in_specs = [tile_spec, tile_spec, tile_spec, _const_spec((LRU_WIDTH + SSD_WIDTH, D_MODEL)),
                _const_spec((1, D_MODEL)), _const_spec((D_MODEL, 4 * D_MODEL))]
    args = [x3, ya, yb, p["even_w_out"], p["odd_norm"], p["odd_w_in"]]
    if kv_init is None:
        kv_spec, kv_shape, aliases = tile_spec, act(BF16), {}
    else:
        kv_spec = pl.BlockSpec((1, tile, D_MODEL), lambda b, i: (b, i + 1, 0))
        kv_shape = jax.ShapeDtypeStruct(kv_init[0].shape, BF16)
        in_specs += [pl.BlockSpec(memory_space=pl.ANY)] * 2
        args += list(kv_init)
        aliases = {len(args) - 2: 2, len(args) - 1: 3}
    return pl.pallas_call(
        _mid_kernel,
        grid=(bn, seq // tile),
        in_specs=in_specs,
        out_specs=[tile_spec, tile_spec, kv_spec, kv_spec, tile_spec],
        out_shape=[act(F32), act(BF16), kv_shape, kv_shape, act(F32)],
        input_output_aliases=aliases,
        compiler_params=_params(2),
        name="even_out_odd_in",
    )(*args)


def _sb_blocks(qns, kbs, vbs, rs, keep):
    n = len(qns)
    tk = kbs[0].shape[0]
    jj = lax.broadcasted_iota(jnp.int32, (tk, tk), 0)
    ss = lax.broadcasted_iota(jnp.int32, (tk, tk), 1)
    later = (jj > ss).astype(BF16)
    zns = [lax.dot_general(qns[c], kbs[c], _NT, preferred_element_type=F32) for c in range(n)]
    lks, logbs, csums = [], [], []
    for c in range(n):
        zn = zns[c]
        lk = jnp.minimum(zn, 0.0) - jnp.log(1.0 + jnp.exp2(jnp.abs(zn) * (-LOG2_E)))
        logbs.append(lk - zn)
        lks.append(jnp.where(keep, lk, 0.0))
    for c in range(n):
        csums.append(jnp.dot(lks[c].astype(BF16), later, preferred_element_type=F32))
    res = []
    for c in range(n):
        logw = logbs[c] + csums[c]
        if rs is not None:
            logw = logw + rs[c]
        wgt = jnp.where(keep, jnp.exp(logw), 0.0)
        out = jnp.dot(wgt.astype(BF16), vbs[c], preferred_element_type=F32)
        total = jnp.sum(lks[c], axis=-1, keepdims=True)
        res.append((out, total if rs is None else rs[c] + total))
    return res


def _attn_kernel(tq, tk, n_tiles, pad, q_ref, k_ref, v_ref, g_ref, o_ref, acc_ref, r_ref):
    i = pl.program_id(2)
    hpt = HEADS_PER_LANE_TILE
    lane = lax.broadcasted_iota(jnp.int32, (tq, LANES), 1)
    head_lanes = [(lane >= h * SB_HEAD_DIM) & (lane < (h + 1) * SB_HEAD_DIM) for h in range(hpt)]
    cols = [slice(c * LANES, (c + 1) * LANES) for c in range(n_tiles)]
    qn = []
    for c in range(n_tiles):
        q = -q_ref[0, :, cols[c]]
        qn.append(jnp.concatenate([jnp.where(m, q, jnp.zeros_like(q)) for m in head_lanes], axis=0))
    q_row = lax.broadcasted_iota(jnp.int32, (tq, tk), 0)
    k_col = lax.broadcasted_iota(jnp.int32, (tq, tk), 1)

    def sweep(start, rs, causal):
        exists = k_col >= (pad - N_META) - start
        keep = exists if causal is None else causal & exists
        keep = jnp.concatenate([keep] * hpt, axis=0)
        rows = pl.ds(pl.multiple_of(start, tq), tk)
        return _sb_blocks(qn, [k_ref[0, rows, cols[c]] for c in range(n_tiles)],
                          [v_ref[0, rows, cols[c]] for c in range(n_tiles)], rs, keep)

    def row_max(res):
        rm = jnp.max(res[0][1])
        for c in range(1, n_tiles):
            rm = jnp.maximum(rm, jnp.max(res[c][1]))
        return rm

    start0 = pad + i * tq - (tk - tq)
    res = sweep(start0, None, k_col < q_row + (tk - tq))
    for c in range(n_tiles):
        acc_ref[c] = res[c][0]
        r_ref[c] = res[c][1]

    def cond(carry):
        start, rm = carry
        return (start + tk > pad - N_META) & (rm > F32_EXP_ZERO_BELOW)

    def body(carry):
        start, _ = carry
        res = sweep(start, [r_ref[c] for c in range(n_tiles)], None)
        for c in range(n_tiles):
            acc_ref[c] += res[c][0]
            r_ref[c] = res[c][1]
        return start - tk, row_max(res)

    lax.while_loop(cond, body, (start0 - tk, row_max(res)))

    for c in range(n_tiles):
        o = acc_ref[c, 0:tq, :]
        for h in range(1, hpt):
            o = jnp.where(head_lanes[h], acc_ref[c, h * tq:(h + 1) * tq, :], o)
        g = g_ref[0, :, cols[c]]
        o_ref[0, :, cols[c]] = (o * (g * _sigmoid(g))).astype(BF16)


def _attn(q, k, v, g, tq, tk, n_tiles, pad):
    bn, seq, width = q.shape
    assert tk % tq == 0 and pad % tq == 0 and pad >= tk and k.shape[1] == pad + seq
    bw = n_tiles * LANES
    grid = (bn, width // bw, seq // tq)
    q_spec = pl.BlockSpec((1, tq, bw), lambda b, p, i: (b, i, p))
    kv_spec = pl.BlockSpec((1, pad + seq, bw), lambda b, p, i: (b, 0, p))
    return pl.pallas_call(
        functools.partial(_attn_kernel, tq, tk, n_tiles, pad),
        grid=grid,
        in_specs=[q_spec, kv_spec, kv_spec, q_spec],
        out_specs=q_spec,
        out_shape=jax.ShapeDtypeStruct((bn, seq, width), BF16),
        scratch_shapes=[pltpu.VMEM((n_tiles, HEADS_PER_LANE_TILE * tq, LANES), F32),
                        pltpu.VMEM((n_tiles, HEADS_PER_LANE_TILE * tq, 1), F32)],
        compiler_params=_params(3),
        name="stick_breaking",
    )(q, k, v, g)


def _final_kernel(h1_ref, o_ref, wo_ref, fw_ref, out_ref):
    h2 = h1_ref[...] + jnp.dot(o_ref[...], wo_ref[...], preferred_element_type=F32)
    out_ref[...] = _rmsnorm(h2, fw_ref[...])


def _final(h1, o, wo, fw, tile):
    rows = h1.shape[0]
    row_spec = pl.BlockSpec((tile, D_MODEL), lambda i: (i, 0))
    return pl.pallas_call(
        _final_kernel,
        grid=(rows // tile,),
        in_specs=[row_spec, row_spec, _const_spec((D_MODEL, D_MODEL)), _const_spec((1, D_MODEL))],
        out_specs=row_spec,
        out_shape=jax.ShapeDtypeStruct((rows, D_MODEL), F32),
        compiler_params=_params(1),
        name="odd_out_final_norm",
    )(h1, o, wo, fw)


def _pair_lanes(vec, chunk):
    return jnp.repeat(vec.astype(F32).reshape(SSD_PAIRS, HEADS_PER_LANE_TILE), chunk, axis=1)


def _layer0_and_qkv(x3, carry, p, row_tile, lru_tile, ssd_tile, chunk, kv_init=None):
    lxc, lg, z, xc, dt, dtp, lru_tail, ssd_tail = _even_in(x3, carry["lru_tail"], carry["ssd_tail"], p,
                                                           row_tile, chunk)
    ya, lru_h = _lru(lxc, lg, carry["lru_h"], p, lru_tile)
    yb, ssd_s = _ssd(xc, z, dt, dtp, carry["ssd_s"], p, _pair_lanes(p["dt_bias"], chunk),
                     _pair_lanes(p["a_log"], chunk), ssd_tile, chunk)
    h1, q, k, v, g = _mid(x3, ya, yb, p, row_tile, kv_init)
    new_carry = {"lru_tail": lru_tail, "lru_h": lru_h, "ssd_tail": ssd_tail, "ssd_s": ssd_s}
    return h1, q, k, v, g, new_carry


def kernel(x, meta, even_norm, even_w_in, lru_conv_w, lru_conv_b, lru_w_a, lru_b_a, lru_w_x, lru_b_x, lru_lambda,
           ssd_conv_w, ssd_conv_b, ssd_dt_bias, ssd_a_log, ssd_d, ssd_norm, even_w_out, odd_norm, odd_w_in,
           odd_w_out, final_norm):
    bsz, seq, _ = x.shape
    n_main = 2 * LRU_WIDTH + SSD_WIDTH + SSD_CONV_DIM
    w_in = even_w_in[0]
    w_dt = w_in[:, n_main:]

    def lane_row(vec):
        return jnp.pad(vec.astype(F32), (0, LANES - vec.shape[0])).reshape(1, LANES)

    p = {
        "even_norm": even_norm[0].reshape(1, -1),
        "w_main": w_in[:, :n_main].astype(BF16),
        "w_dt": jnp.pad(w_dt, ((0, 0), (0, LANES - SSD_HEADS))).astype(BF16),
        "w_dt_even": w_dt[:, 0::2].T.astype(BF16), "w_dt_odd": w_dt[:, 1::2].T.astype(BF16),
        "lru_conv_w": lru_conv_w[0], "lru_conv_b": lru_conv_b[0].reshape(1, -1),
        "lru_w_a": lru_w_a[0].astype(BF16), "lru_b_a": lru_b_a[0].reshape(1, -1),
        "lru_w_x": lru_w_x[0].astype(BF16), "lru_b_x": lru_b_x[0].reshape(1, -1),
        "lru_lambda": lru_lambda[0].reshape(1, -1),
        "ssd_conv_w": ssd_conv_w[0], "ssd_conv_b": ssd_conv_b[0].reshape(1, -1),
        "dt_bias": ssd_dt_bias[0], "a_log": ssd_a_log[0],
        "dt_bias_row": lane_row(ssd_dt_bias[0]), "a_log_row": lane_row(ssd_a_log[0]),
        "d_expanded": jnp.repeat(ssd_d[0], SSD_HEAD_DIM).reshape(1, -1),
        "ssd_norm": ssd_norm[0].reshape(1, -1),
        "even_w_out": even_w_out[0].astype(BF16),
        "odd_norm": odd_norm[0].reshape(1, -1),
        "odd_w_in": odd_w_in[0].astype(BF16),
    }
    zero_carry = {
        "lru_tail": jnp.zeros((1, CONV_HALO, LRU_WIDTH), F32), "lru_h": jnp.zeros((1, 1, LRU_WIDTH), F32),
        "ssd_tail": jnp.zeros((1, CONV_HALO, SSD_CONV_DIM), F32),
        "ssd_s": jnp.zeros((1, SSD_GROUPS, SSD_STATE, SSD_GROUP_WIDTH), F32),
    }
    _, _, km, vm, _, carry = _layer0_and_qkv(meta.astype(x.dtype)[None], zero_carry, p,
                                             row_tile=N_META, lru_tile=N_META, ssd_tile=N_META, chunk=N_META)
    row_tile = min(512, seq)

    def with_meta_front(m):
        front = jnp.zeros((bsz, row_tile + seq, D_MODEL), BF16)
        return lax.dynamic_update_slice(front, jnp.broadcast_to(m, (bsz, N_META, D_MODEL)),
                                        (0, row_tile - N_META, 0))

    h1, q, k, v, g, _ = _layer0_and_qkv(x, carry, p, row_tile=row_tile, lru_tile=min(256, seq),
                                        ssd_tile=min(512, seq), chunk=64,
                                        kv_init=(with_meta_front(km), with_meta_front(vm)))
    o = _attn(q, k, v, g, tq=64, tk=256, n_tiles=8, pad=row_tile)
    out = _final(h1.reshape(bsz * seq, D_MODEL), o.reshape(bsz * seq, D_MODEL), odd_w_out[0].astype(BF16),
                 final_norm.reshape(1, -1), row_tile)
    return out.reshape(bsz, seq, D_MODEL)
```

```python
import functools

import jax
import jax.numpy as jnp
from jax import lax
from jax.experimental import pallas as pl
from jax.experimental.pallas import tpu as pltpu

F32 = jnp.float32
BF16 = jnp.bfloat16

D_MODEL = 1024
N_META = 16
EPS = 1e-6
CONV_W = 4
LRU_WIDTH = 1024
LRU_BLOCKS = 4
LRU_BLOCK = LRU_WIDTH // LRU_BLOCKS
RG_LRU_C = 8.0
SSD_WIDTH = 1024
SSD_HEAD_DIM = 64
SSD_HEADS = 16
SSD_GROUPS = 2
SSD_HPG = SSD_HEADS // SSD_GROUPS
SSD_STATE = 128
SSD_CONV_DIM = SSD_WIDTH + 2 * SSD_GROUPS * SSD_STATE
SSD_GROUP_WIDTH = SSD_WIDTH // SSD_GROUPS
SB_HEADS = 16
SB_HEAD_DIM = 64
LANES = 128
HEADS_PER_LANE_TILE = LANES // SB_HEAD_DIM
SSD_PAIRS = SSD_HEADS // HEADS_PER_LANE_TILE
SSD_PAIRS_PER_GROUP = SSD_PAIRS // SSD_GROUPS
CONV_HALO = 8
F32_EXP_ZERO_BELOW = -104.0
LOG2_E = 1.4426950408889634
VMEM_LIMIT = 56 * 1024 * 1024

_NT = (((1,), (1,)), ((), ()))
_TN = (((0,), (0,)), ((), ()))
_EXACT = lax.Precision.HIGHEST


def _params(n_grid):
    return pltpu.CompilerParams(dimension_semantics=("arbitrary",) * n_grid,
                                vmem_limit_bytes=VMEM_LIMIT)


def _const_spec(shape):
    zeros = (0,) * len(shape)
    return pl.BlockSpec(shape, lambda *_: zeros)


def _silu(x):
    half = 0.5 * x
    return half * (jnp.tanh(half) + 1.0)


def _softplus(x):
    return jnp.maximum(x, 0.0) + jnp.log(1.0 + jnp.exp(-jnp.abs(x)))


def _rmsnorm(x, w):
    ms = jnp.mean(x * x, axis=-1, keepdims=True)
    return (x * lax.rsqrt(ms + EPS)) * w


def _causal_conv(x, prev, cw, cb):
    assert x.shape[0] > CONV_HALO
    acc = cb + cw[CONV_W - 1:CONV_W, :] * x
    row = lax.broadcasted_iota(jnp.int32, (CONV_HALO, x.shape[1]), 0)
    for back in range(1, CONV_W):
        tap = CONV_W - 1 - back
        rolled = pltpu.roll(x, back, axis=0)
        head = jnp.where(row < back, pltpu.roll(prev, back, axis=0), rolled[0:CONV_HALO, :])
        acc = acc + cw[tap:tap + 1, :] * jnp.concatenate([head, rolled[CONV_HALO:, :]], axis=0)
    return acc


def _even_in_kernel(t, chunk, x_ref, ltail_ref, xtail_ref, nw_ref, w_ref, wdt_ref, wdte_ref, wdto_ref,
                    lcw_ref, lcb_ref, xcw_ref, xcb_ref,
                    lxc_ref, lg_ref, z_ref, xc_ref, dt_ref, dtp_ref, ltail_out_ref, xtail_out_ref, lbuf, xbuf):
    i = pl.program_id(1)

    @pl.when(i == 0)
    def _():
        lbuf[...] = ltail_ref[0]
        xbuf[...] = xtail_ref[0]

    u = _rmsnorm(x_ref[0], nw_ref[...]).astype(BF16)
    c0, c1, c2, c3 = LRU_WIDTH, 2 * LRU_WIDTH, 2 * LRU_WIDTH + SSD_WIDTH, 2 * LRU_WIDTH + SSD_WIDTH + SSD_CONV_DIM
    lx = jnp.dot(u, w_ref[:, 0:c0], preferred_element_type=F32)
    lxc_ref[0] = _causal_conv(lx, lbuf[...], lcw_ref[...], lcb_ref[...])
    lg_ref[0] = jnp.dot(u, w_ref[:, c0:c1], preferred_element_type=F32)
    z_ref[0] = jnp.dot(u, w_ref[:, c1:c2], preferred_element_type=F32)
    xbc = jnp.dot(u, w_ref[:, c2:c3], preferred_element_type=F32)
    conv = _causal_conv(xbc, xbuf[...], xcw_ref[...], xcb_ref[...])
    xc_ref[0] = _silu(conv)
    dt_ref[0] = jnp.dot(u, wdt_ref[...], preferred_element_type=F32)
    for c in range(t // chunk):
        uc = u[c * chunk:(c + 1) * chunk, :]
        even = lax.dot_general(wdte_ref[...], uc, _NT, preferred_element_type=F32)
        odd = lax.dot_general(wdto_ref[...], uc, _NT, preferred_element_type=F32)
        dtp_ref[c] = jnp.concatenate([even, odd], axis=1)

    lbuf[...] = lx[t - CONV_HALO:t, :]
    xbuf[...] = xbc[t - CONV_HALO:t, :]

    @pl.when(i == pl.num_programs(1) - 1)
    def _():
        ltail_out_ref[0] = lbuf[...]
        xtail_out_ref[0] = xbuf[...]


def _even_in(x3, ltail, xtail, p, tile, chunk):
    bn, seq, _ = x3.shape
    nt = seq // tile
    n_main = p["w_main"].shape[1]
    tile_spec = lambda c: pl.BlockSpec((1, tile, c), lambda b, i: (b, i, 0))
    shared = lambda c: pl.BlockSpec((1, CONV_HALO, c), lambda b, i: (0, 0, 0))
    per_b = lambda c: pl.BlockSpec((1, CONV_HALO, c), lambda b, i: (b, 0, 0))
    act = lambda c, dt=F32: jax.ShapeDtypeStruct((bn, seq, c), dt)
    return pl.pallas_call(
        functools.partial(_even_in_kernel, tile, chunk),
        grid=(bn, nt),
        in_specs=[tile_spec(D_MODEL), shared(LRU_WIDTH), shared(SSD_CONV_DIM),
                  _const_spec((1, D_MODEL)), _const_spec((D_MODEL, n_main)), _const_spec((D_MODEL, LANES)),
                  _const_spec((SSD_PAIRS, D_MODEL)), _const_spec((SSD_PAIRS, D_MODEL)),
                  _const_spec((CONV_W, LRU_WIDTH)), _const_spec((1, LRU_WIDTH)),
                  _const_spec((CONV_W, SSD_CONV_DIM)), _const_spec((1, SSD_CONV_DIM))],
        out_specs=[tile_spec(LRU_WIDTH), tile_spec(LRU_WIDTH), tile_spec(SSD_WIDTH), tile_spec(SSD_CONV_DIM),
                   tile_spec(LANES),
                   pl.BlockSpec((tile // chunk, SSD_PAIRS, 2 * chunk), lambda b, i: (b * nt + i, 0, 0)),
                   per_b(LRU_WIDTH), per_b(SSD_CONV_DIM)],
        out_shape=[act(LRU_WIDTH), act(LRU_WIDTH), act(SSD_WIDTH), act(SSD_CONV_DIM), act(LANES),
                   jax.ShapeDtypeStruct((bn * seq // chunk, SSD_PAIRS, 2 * chunk), F32),
                   jax.ShapeDtypeStruct((bn, CONV_HALO, LRU_WIDTH), F32),
                   jax.ShapeDtypeStruct((bn, CONV_HALO, SSD_CONV_DIM), F32)],
        scratch_shapes=[pltpu.VMEM((CONV_HALO, LRU_WIDTH), F32), pltpu.VMEM((CONV_HALO, SSD_CONV_DIM), F32)],
        compiler_params=_params(2),
        name="even_in",
    )(x3, ltail, xtail, p["even_norm"], p["w_main"], p["w_dt"], p["w_dt_even"], p["w_dt_odd"],
      p["lru_conv_w"], p["lru_conv_b"], p["ssd_conv_w"], p["ssd_conv_b"])


def _lru_kernel(t, lx_ref, lg_ref, h0_ref, wa_ref, ba_ref, wx_ref, bx_ref, lam_ref,
                ya_ref, h_out_ref, abuf, bbuf, hbuf, hcar):
    i = pl.program_id(1)

    @pl.when(i == 0)
    def _():
        hcar[...] = h0_ref[0]

    lx = lx_ref[0]
    lxb = lx.astype(BF16)
    for g in range(LRU_BLOCKS):
        sl = slice(g * LRU_BLOCK, (g + 1) * LRU_BLOCK)
        xg = lxb[:, sl]
        tr = jnp.tanh(jnp.dot(xg, wa_ref[g], preferred_element_type=F32) + ba_ref[:, sl])
        ti = jnp.tanh(jnp.dot(xg, wx_ref[g], preferred_element_type=F32) + bx_ref[:, sl])
        half_c_sp = (-0.5 * RG_LRU_C) * _softplus(-lam_ref[:, sl])
        a = jnp.exp(half_c_sp * tr + half_c_sp)
        mult = jnp.sqrt(1.0 - a * a)
        abuf[:, sl] = a
        bbuf[:, sl] = (mult * (ti + 1.0)) * (0.5 * lx[:, sl])

    def step(row, h):
        h = abuf[pl.ds(row, 1), :] * h + bbuf[pl.ds(row, 1), :]
        hbuf[pl.ds(row, 1), :] = h
        return h

    h = lax.fori_loop(0, t, step, hcar[...], unroll=8)
    hcar[...] = h
    ya_ref[0] = (hbuf[...] * _silu(lg_ref[0])).astype(BF16)

    @pl.when(i == pl.num_programs(1) - 1)
    def _():
        h_out_ref[0] = h


def _lru(lxc, lg, h0, p, tile):
    bn, seq, _ = lxc.shape
    tile_spec = pl.BlockSpec((1, tile, LRU_WIDTH), lambda b, i: (b, i, 0))
    return pl.pallas_call(
        functools.partial(_lru_kernel, tile),
        grid=(bn, seq // tile),
        in_specs=[tile_spec, tile_spec, pl.BlockSpec((1, 1, LRU_WIDTH), lambda b, i: (0, 0, 0)),
                  _const_spec((LRU_BLOCKS, LRU_BLOCK, LRU_BLOCK)), _const_spec((1, LRU_WIDTH)),
                  _const_spec((LRU_BLOCKS, LRU_BLOCK, LRU_BLOCK)), _const_spec((1, LRU_WIDTH)),
                  _const_spec((1, LRU_WIDTH))],
        out_specs=[tile_spec, pl.BlockSpec((1, 1, LRU_WIDTH), lambda b, i: (b, 0, 0))],
        out_shape=[jax.ShapeDtypeStruct((bn, seq, LRU_WIDTH), BF16),
                   jax.ShapeDtypeStruct((bn, 1, LRU_WIDTH), F32)],
        scratch_shapes=[pltpu.VMEM((tile, LRU_WIDTH), F32), pltpu.VMEM((tile, LRU_WIDTH), F32),
                        pltpu.VMEM((tile, LRU_WIDTH), F32), pltpu.VMEM((1, LRU_WIDTH), F32)],
        compiler_params=_params(2),
        name="rg_lru",
    )(lxc, lg, h0, p["lru_w_a"], p["lru_b_a"], p["lru_w_x"], p["lru_b_x"], p["lru_lambda"])


def _ssd_kernel(t, q, xc_ref, z_ref, dt_ref, dtp_ref, s0_ref, dtb_ref, dtbp_ref, alog_ref, alogp_ref,
                dexp_ref, nrm_ref, yb_ref, s_out_ref, ybuf, sbuf):
    i = pl.program_id(1)

    @pl.when(i == 0)
    def _():
        sbuf[...] = s0_ref[0]

    a_row = -jnp.exp(alog_ref[...])
    a_pk = -jnp.exp(alogp_ref[...])
    rr = lax.broadcasted_iota(jnp.int32, (q, q), 0)
    cc = lax.broadcasted_iota(jnp.int32, (q, q), 1)
    tril = (cc <= rr).astype(F32)
    r2 = lax.broadcasted_iota(jnp.int32, (2 * q, 2 * q), 0)
    c2 = lax.broadcasted_iota(jnp.int32, (2 * q, 2 * q), 1)
    bd = ((r2 <= c2) & ((r2 < q) == (c2 < q))).astype(F32)
    lane_x = lax.broadcasted_iota(jnp.int32, (q, LANES), 1)
    first_x = lane_x < SSD_HEAD_DIM
    lane_m = lax.broadcasted_iota(jnp.int32, (q, 2 * q), 1)
    first_m = lane_m < q
    row_m = lax.broadcasted_iota(jnp.int32, (q, 2 * q), 0)
    causal_m = jnp.where(first_m, lane_m, lane_m - q) <= row_m

    def chunk(c, carry):
        rows = pl.ds(pl.multiple_of(c * q, q), q)
        dt = _softplus(dt_ref[0, rows, :] + dtb_ref[...])
        dtp = _softplus(dtp_ref[c] + dtbp_ref[...])
        acum = jnp.dot(tril, dt * a_row, precision=_EXACT, preferred_element_type=F32)
        acum_pk = jnp.dot(dtp * a_pk, bd, precision=_EXACT, preferred_element_type=F32)
        for g in range(SSD_GROUPS):
            b_lo = SSD_WIDTH + g * SSD_STATE
            c_lo = SSD_WIDTH + (SSD_GROUPS + g) * SSD_STATE
            b_g = xc_ref[0, rows, b_lo:b_lo + SSD_STATE].astype(BF16)
            c_g = xc_ref[0, rows, c_lo:c_lo + SSD_STATE].astype(BF16)
            cbb = lax.dot_general(c_g, jnp.concatenate([b_g, b_g], axis=0), _NT,
                                  preferred_element_type=F32)
            y_diag, wx, e_acum, e_tot = [], [], [], []
            for pp in range(SSD_PAIRS_PER_GROUP):
                pr = g * SSD_PAIRS_PER_GROUP + pp
                h0, h1 = 2 * pr, 2 * pr + 1
                acol_x = jnp.where(first_x, acum[:, h0:h0 + 1], acum[:, h1:h1 + 1])
                dt_x = jnp.where(first_x, dt[:, h0:h0 + 1], dt[:, h1:h1 + 1])
                if 2 * q == LANES:
                    acol_m = acol_x
                else:
                    acol_m = jnp.where(first_m, acum[:, h0:h0 + 1], acum[:, h1:h1 + 1])
                xdt = xc_ref[0, rows, pr * LANES:(pr + 1) * LANES] * dt_x
                xdt_b = xdt.astype(BF16)
                decay = jnp.where(causal_m, jnp.exp(acol_m - acum_pk[pr:pr + 1, :]), 0.0)
                m = (cbb * decay).astype(BF16)
                zeros = jnp.zeros_like(xdt_b)
                rhs = jnp.concatenate([jnp.where(first_x, xdt_b, zeros), jnp.where(first_x, zeros, xdt_b)],
                                      axis=0)
                y_diag.append(jnp.dot(m, rhs, preferred_element_type=F32))
                tot_x = acol_x[q - 1:q, :]
                wx.append((jnp.exp(tot_x - acol_x) * xdt).astype(BF16))
                e_acum.append(jnp.exp(acol_x))
                e_tot.append(jnp.exp(tot_x))
            state = sbuf[g]
            y_off = jnp.dot(c_g, state.astype(BF16), preferred_element_type=F32)
            cols = slice(g * SSD_GROUP_WIDTH, (g + 1) * SSD_GROUP_WIDTH)
            ybuf[rows, cols] = jnp.concatenate(y_diag, axis=1) + y_off * jnp.concatenate(e_acum, axis=1)
            sbuf[g] = jnp.concatenate(e_tot, axis=1) * state + lax.dot_general(
                b_g, jnp.concatenate(wx, axis=1), _TN, preferred_element_type=F32)
        return carry

    lax.fori_loop(0, t // q, chunk, 0, unroll=min(2, t // q))

    y = ybuf[...] + xc_ref[0, :, 0:SSD_WIDTH] * dexp_ref[...]
    zz = z_ref[0]
    gt = y * _silu(zz)
    for g in range(SSD_GROUPS):
        cols = slice(g * SSD_GROUP_WIDTH, (g + 1) * SSD_GROUP_WIDTH)
        gg = gt[:, cols]
        ms = jnp.mean(gg * gg, axis=-1, keepdims=True)
        yb_ref[0, :, cols] = (gg * lax.rsqrt(ms + EPS) * nrm_ref[:, cols]).astype(BF16)

    @pl.when(i == pl.num_programs(1) - 1)
    def _():
        s_out_ref[0] = sbuf[...]


def _ssd(xc, z, dt, dtp, s0, p, dtbp, alogp, tile, chunk):
    bn, seq, _ = xc.shape
    nt = seq // tile
    tile_spec = lambda c: pl.BlockSpec((1, tile, c), lambda b, i: (b, i, 0))
    state_shape = (SSD_GROUPS, SSD_STATE, SSD_GROUP_WIDTH)
    return pl.pallas_call(
        functools.partial(_ssd_kernel, tile, chunk),
        grid=(bn, nt),
        in_specs=[tile_spec(SSD_CONV_DIM), tile_spec(SSD_WIDTH), tile_spec(LANES),
                  pl.BlockSpec((tile // chunk, SSD_PAIRS, 2 * chunk), lambda b, i: (b * nt + i, 0, 0)),
                  pl.BlockSpec((1,) + state_shape, lambda b, i: (0, 0, 0, 0)),
                  _const_spec((1, LANES)), _const_spec((SSD_PAIRS, 2 * chunk)),
                  _const_spec((1, LANES)), _const_spec((SSD_PAIRS, 2 * chunk)),
                  _const_spec((1, SSD_WIDTH)), _const_spec((1, SSD_WIDTH))],
        out_specs=[tile_spec(SSD_WIDTH), pl.BlockSpec((1,) + state_shape, lambda b, i: (b, 0, 0, 0))],
        out_shape=[jax.ShapeDtypeStruct((bn, seq, SSD_WIDTH), BF16),
                   jax.ShapeDtypeStruct((bn,) + state_shape, F32)],
        scratch_shapes=[pltpu.VMEM((tile, SSD_WIDTH), F32), pltpu.VMEM(state_shape, F32)],
        compiler_params=_params(2),
        name="ssd",
    )(xc, z, dt, dtp, s0, p["dt_bias_row"], dtbp, p["a_log_row"], alogp, p["d_expanded"], p["ssd_norm"])


def _mid_body(x_ref, ya_ref, yb_ref, wo_ref, nw_ref, wi_ref, h1_ref, q_ref, k_ref, v_ref, g_ref):
    h1 = (x_ref[0] + jnp.dot(ya_ref[0], wo_ref[0:LRU_WIDTH, :], preferred_element_type=F32)
          + jnp.dot(yb_ref[0], wo_ref[LRU_WIDTH:LRU_WIDTH + SSD_WIDTH, :], preferred_element_type=F32))
    h1_ref[0] = h1
    u = _rmsnorm(h1, nw_ref[...]).astype(BF16)
    w = D_MODEL
    q_ref[0] = (jnp.dot(u, wi_ref[:, 0:w], preferred_element_type=F32) * (SB_HEAD_DIM ** -0.5)).astype(BF16)
    k_ref[0] = jnp.dot(u, wi_ref[:, w:2 * w], preferred_element_type=F32).astype(BF16)
    v_ref[0] = jnp.dot(u, wi_ref[:, 2 * w:3 * w], preferred_element_type=F32).astype(BF16)
    g_ref[0] = jnp.dot(u, wi_ref[:, 3 * w:4 * w], preferred_element_type=F32)


def _mid_front_kernel(tile, x_ref, ya_ref, yb_ref, wo_ref, nw_ref, wi_ref, km_ref, vm_ref,
                      h1_ref, q_ref, k_ref, v_ref, g_ref):
    i = pl.program_id(1)

    @pl.when(i == 0)
    def _():
        for dst, src in ((k_ref, km_ref), (v_ref, vm_ref)):
            dst[0, 0:tile - N_META, :] = jnp.zeros((tile - N_META, D_MODEL), BF16)
            dst[0, tile - N_META:tile, :] = src[0]

    @pl.when(i > 0)
    def _():
        _mid_body(x_ref, ya_ref, yb_ref, wo_ref, nw_ref, wi_ref, h1_ref, q_ref, k_ref, v_ref, g_ref)


def _mid(x3, ya, yb, p, tile, meta_kv=None):
    bn, seq, _ = x3.shape
    nt = seq // tile
    act = lambda dt, rows=seq: jax.ShapeDtypeStruct((bn, rows, D_MODEL), dt)
    weights = [_const_spec((LRU_WIDTH + SSD_WIDTH, D_MODEL)), _const_spec((1, D_MODEL)),
               _const_spec((D_MODEL, 4 * D_MODEL))]
    args = [x3, ya, yb, p["even_w_out"], p["odd_norm"], p["odd_w_in"]]
    if meta_kv is None:
        tile_spec = pl.BlockSpec((1, tile, D_MODEL), lambda b, i: (b, i, 0))
        return pl.pallas_call(
            _mid_body,
            grid=(bn, nt),
            in_specs=[tile_spec] * 3 + weights,
            out_specs=[tile_spec] * 5,
            out_shape=[act(F32), act(BF16), act(BF16), act(BF16), act(F32)],
            compiler_params=_params(2),
            name="even_out_odd_in",
        )(*args)
    tile_spec = pl.BlockSpec((1, tile, D_MODEL), lambda b, i: (b, jnp.maximum(i - 1, 0), 0))
    kv_spec = pl.BlockSpec((1, tile, D_MODEL), lambda b, i: (b, i, 0))
    meta_spec = pl.BlockSpec((1, N_META, D_MODEL), lambda b, i: (0, 0, 0))
    return pl.pallas_call(
        functools.partial(_mid_front_kernel, tile),
        grid=(bn, nt + 1),
        in_specs=[tile_spec] * 3 + weights + [meta_spec] * 2,
        out_specs=[tile_spec, tile_spec, kv_spec, kv_spec, tile_spec],
        out_shape=[act(F32), act(BF16), act(BF16, tile + seq), act(BF16, tile + seq), act(F32)],
        compiler_params=_params(2),
        name="even_out_odd_in",
    )(*args, *meta_kv)


def _sb_blocks(qns, kbs, vbs, rs, keep):
    n = len(qns)
    tk = kbs[0].shape[0]
    jj = lax.broadcasted_iota(jnp.int32, (tk, tk), 0)
    ss = lax.broadcasted_iota(jnp.int32, (tk, tk), 1)
    later = (jj > ss).astype(BF16)
    zns = [lax.dot_general(qns[c], kbs[c], _NT, preferred_element_type=F32) for c in range(n)]
    lks, logbs, csums = [], [], []
    for c in range(n):
        zn = zns[c]
        lk = jnp.minimum(zn, 0.0) - jnp.log(1.0 + jnp.exp2(jnp.abs(zn) * (-LOG2_E)))
        logbs.append(lk - zn)
        lks.append(jnp.where(keep, lk, 0.0))
    for c in range(n):
        csums.append(jnp.dot(lks[c].astype(BF16), later, preferred_element_type=F32))
    res = []
    for c in range(n):
        logw = logbs[c] + csums[c]
        if rs is not None:
            logw = logw + rs[c]
        wgt = jnp.where(keep, jnp.exp(logw), 0.0)
        out = jnp.dot(wgt.astype(BF16), vbs[c], preferred_element_type=F32)
        total = jnp.sum(lks[c], axis=-1, keepdims=True)
        res.append((out, total if rs is None else rs[c] + total))
    return res


def _attn_kernel(tq, tk, n_tiles, pad, q_ref, k_ref, v_ref, g_ref, o_ref, acc_ref, r_ref):
    i = pl.program_id(2)
    hpt = HEADS_PER_LANE_TILE
    lane = lax.broadcasted_iota(jnp.int32, (tq, LANES), 1)
    head_lanes = [(lane >= h * SB_HEAD_DIM) & (lane < (h + 1) * SB_HEAD_DIM) for h in range(hpt)]
    cols = [slice(c * LANES, (c + 1) * LANES) for c in range(n_tiles)]
    qn = []
    for c in range(n_tiles):
        q = -q_ref[0, :, cols[c]]
        qn.append(jnp.concatenate([jnp.where(m, q, jnp.zeros_like(q)) for m in head_lanes], axis=0))
    q_row = lax.broadcasted_iota(jnp.int32, (tq, tk), 0)
    k_col = lax.broadcasted_iota(jnp.int32, (tq, tk), 1)

    def sweep(start, rs, causal):
        exists = k_col >= (pad - N_META) - start
        keep = exists if causal is None else causal & exists
        keep = jnp.concatenate([keep] * hpt, axis=0)
        rows = pl.ds(pl.multiple_of(start, tq), tk)
        return _sb_blocks(qn, [k_ref[0, rows, cols[c]] for c in range(n_tiles)],
                          [v_ref[0, rows, cols[c]] for c in range(n_tiles)], rs, keep)

    def row_max(res):
        rm = jnp.max(res[0][1])
        for c in range(1, n_tiles):
            rm = jnp.maximum(rm, jnp.max(res[c][1]))
        return rm

    start0 = pad + i * tq - (tk - tq)
    res = sweep(start0, None, k_col < q_row + (tk - tq))
    for c in range(n_tiles):
        acc_ref[c] = res[c][0]
        r_ref[c] = res[c][1]

    def cond(carry):
        start, rm = carry
        return (start + tk > pad - N_META) & (rm > F32_EXP_ZERO_BELOW)

    def body(carry):
        start, _ = carry
        res = sweep(start, [r_ref[c] for c in range(n_tiles)], None)
        for c in range(n_tiles):
            acc_ref[c] += res[c][0]
            r_ref[c] = res[c][1]
        return start - tk, row_max(res)

    lax.while_loop(cond, body, (start0 - tk, row_max(res)))

    for c in range(n_tiles):
        o = acc_ref[c, 0:tq, :]
        for h in range(1, hpt):
            o = jnp.where(head_lanes[h], acc_ref[c, h * tq:(h + 1) * tq, :], o)
        g = g_ref[0, :, cols[c]]
        o_ref[0, :, cols[c]] = (o * _silu(g)).astype(BF16)


def _attn(q, k, v, g, tq, tk, n_tiles, pad):
    bn, seq, width = q.shape
    assert tk % tq == 0 and pad % tq == 0 and pad >= tk and k.shape[1] == pad + seq
    bw = n_tiles * LANES
    grid = (bn, width // bw, seq // tq)
    q_spec = pl.BlockSpec((1, tq, bw), lambda b, p, i: (b, i, p))
    kv_spec = pl.BlockSpec((1, pad + seq, bw), lambda b, p, i: (b, 0, p))
    return pl.pallas_call(
        functools.partial(_attn_kernel, tq, tk, n_tiles, pad),
        grid=grid,
        in_specs=[q_spec, kv_spec, kv_spec, q_spec],
        out_specs=q_spec,
        out_shape=jax.ShapeDtypeStruct((bn, seq, width), BF16),
        scratch_shapes=[pltpu.VMEM((n_tiles, HEADS_PER_LANE_TILE * tq, LANES), F32),
                        pltpu.VMEM((n_tiles, HEADS_PER_LANE_TILE * tq, 1), F32)],
        compiler_params=_params(3),
        name="stick_breaking",
    )(q, k, v, g)


def _final_kernel(h1_ref, o_ref, wo_ref, fw_ref, out_ref):
    h2 = h1_ref[...] + jnp.dot(o_ref[...], wo_ref[...], preferred_element_type=F32)
    out_ref[...] = _rmsnorm(h2, fw_ref[...])


def _final(h1, o, wo, fw, tile):
    rows = h1.shape[0]
    row_spec = pl.BlockSpec((tile, D_MODEL), lambda i: (i, 0))
    return pl.pallas_call(
        _final_kernel,
        grid=(rows // tile,),
        in_specs=[row_spec, row_spec, _const_spec((D_MODEL, D_MODEL)), _const_spec((1, D_MODEL))],
        out_specs=row_spec,
        out_shape=jax.ShapeDtypeStruct((rows, D_MODEL), F32),
        compiler_params=_params(1),
        name="odd_out_final_norm",
    )(h1, o, wo, fw)


def _pair_lanes(vec, chunk):
    return jnp.repeat(vec.astype(F32).reshape(SSD_PAIRS, HEADS_PER_LANE_TILE), chunk, axis=1)


def _layer0_and_qkv(x3, carry, p, row_tile, lru_tile, ssd_tile, chunk, meta_kv=None):
    lxc, lg, z, xc, dt, dtp, lru_tail, ssd_tail = _even_in(x3, carry["lru_tail"], carry["ssd_tail"], p,
                                                           row_tile, chunk)
    ya, lru_h = _lru(lxc, lg, carry["lru_h"], p, lru_tile)
    yb, ssd_s = _ssd(xc, z, dt, dtp, carry["ssd_s"], p, _pair_lanes(p["dt_bias"], chunk),
                     _pair_lanes(p["a_log"], chunk), ssd_tile, chunk)
    h1, q, k, v, g = _mid(x3, ya, yb, p, row_tile, meta_kv)
    new_carry = {"lru_tail": lru_tail, "lru_h": lru_h, "ssd_tail": ssd_tail, "ssd_s": ssd_s}
    return h1, q, k, v, g, new_carry


def kernel(x, meta, even_norm, even_w_in, lru_conv_w, lru_conv_b, lru_w_a, lru_b_a, lru_w_x, lru_b_x, lru_lambda,
           ssd_conv_w, ssd_conv_b, ssd_dt_bias, ssd_a_log, ssd_d, ssd_norm, even_w_out, odd_norm, odd_w_in,
           odd_w_out, final_norm):
    bsz, seq, _ = x.shape
    n_main = 2 * LRU_WIDTH + SSD_WIDTH + SSD_CONV_DIM
    w_in = even_w_in[0]
    w_dt = w_in[:, n_main:]

    def lane_row(vec):
        return jnp.pad(vec.astype(F32), (0, LANES - vec.shape[0])).reshape(1, LANES)

    p = {
        "even_norm": even_norm[0].reshape(1, -1),
        "w_main": w_in[:, :n_main].astype(BF16),
        "w_dt": jnp.pad(w_dt, ((0, 0), (0, LANES - SSD_HEADS))).astype(BF16),
        "w_dt_even": w_dt[:, 0::2].T.astype(BF16), "w_dt_odd": w_dt[:, 1::2].T.astype(BF16),
        "lru_conv_w": lru_conv_w[0], "lru_conv_b": lru_conv_b[0].reshape(1, -1),
        "lru_w_a": (0.5 * lru_w_a[0]).astype(BF16), "lru_b_a": 0.5 * lru_b_a[0].reshape(1, -1),
        "lru_w_x": (0.5 * lru_w_x[0]).astype(BF16), "lru_b_x": 0.5 * lru_b_x[0].reshape(1, -1),
        "lru_lambda": lru_lambda[0].reshape(1, -1),
        "ssd_conv_w": ssd_conv_w[0], "ssd_conv_b": ssd_conv_b[0].reshape(1, -1),
        "dt_bias": ssd_dt_bias[0], "a_log": ssd_a_log[0],
        "dt_bias_row": lane_row(ssd_dt_bias[0]), "a_log_row": lane_row(ssd_a_log[0]),
        "d_expanded": jnp.repeat(ssd_d[0], SSD_HEAD_DIM).reshape(1, -1),
        "ssd_norm": ssd_norm[0].reshape(1, -1),
        "even_w_out": even_w_out[0].astype(BF16),
        "odd_norm": odd_norm[0].reshape(1, -1),
        "odd_w_in": odd_w_in[0].astype(BF16),
    }
    zero_carry = {
        "lru_tail": jnp.zeros((1, CONV_HALO, LRU_WIDTH), F32), "lru_h": jnp.zeros((1, 1, LRU_WIDTH), F32),
        "ssd_tail": jnp.zeros((1, CONV_HALO, SSD_CONV_DIM), F32),
        "ssd_s": jnp.zeros((1, SSD_GROUPS, SSD_STATE, SSD_GROUP_WIDTH), F32),
    }
    _, _, km, vm, _, carry = _layer0_and_qkv(meta.astype(x.dtype)[None], zero_carry, p,
                                             row_tile=N_META, lru_tile=N_META, ssd_tile=N_META, chunk=N_META)
    row_tile = min(512, seq)
    h1, q, k, v, g, _ = _layer0_and_qkv(x, carry, p, row_tile=row_tile, lru_tile=min(512, seq),
                                        ssd_tile=min(512, seq), chunk=64, meta_kv=(km, vm))
    o = _attn(q, k, v, g, tq=64, tk=256, n_tiles=8, pad=row_tile)
    out = _final(h1.reshape(bsz * seq, D_MODEL), o.reshape(bsz * seq, D_MODEL), odd_w_out[0].astype(BF16),
                 final_norm.reshape(1, -1), row_tile)
    return out.reshape(bsz, seq, D_MODEL)
```

```python
import functools

import jax
import jax.numpy as jnp
from jax import lax
from jax.experimental import pallas as pl
from jax.experimental.pallas import tpu as pltpu

F32 = jnp.float32
BF16 = jnp.bfloat16

D_MODEL = 1024
N_META = 16
EPS = 1e-6
CONV_W = 4
LRU_WIDTH = 1024
LRU_BLOCKS = 4
LRU_BLOCK = LRU_WIDTH // LRU_BLOCKS
RG_LRU_C = 8.0
SSD_WIDTH = 1024
SSD_HEAD_DIM = 64
SSD_HEADS = 16
SSD_GROUPS = 2
SSD_HPG = SSD_HEADS // SSD_GROUPS
SSD_STATE = 128
SSD_CONV_DIM = SSD_WIDTH + 2 * SSD_GROUPS * SSD_STATE
SSD_GROUP_WIDTH = SSD_WIDTH // SSD_GROUPS
SB_HEADS = 16
SB_HEAD_DIM = 64
LANES = 128
HEADS_PER_LANE_TILE = LANES // SB_HEAD_DIM
SSD_PAIRS = SSD_HEADS // HEADS_PER_LANE_TILE
SSD_PAIRS_PER_GROUP = SSD_PAIRS // SSD_GROUPS
CONV_HALO = 8
F32_EXP_ZERO_BELOW = -104.0
LOG2_E = 1.4426950408889634
VMEM_LIMIT = 56 * 1024 * 1024

_NT = (((1,), (1,)), ((), ()))
_TN = (((0,), (0,)), ((), ()))
_EXACT = lax.Precision.HIGHEST


def _params(n_grid):
    return pltpu.CompilerParams(dimension_semantics=("arbitrary",) * n_grid,
                                vmem_limit_bytes=VMEM_LIMIT)


def _const_spec(shape):
    zeros = (0,) * len(shape)
    return pl.BlockSpec(shape, lambda *_: zeros)


def _silu(x):
    half = 0.5 * x
    return half * (jnp.tanh(half) + 1.0)


def _softplus(x):
    return jnp.maximum(x, 0.0) + jnp.log(1.0 + jnp.exp(-jnp.abs(x)))


def _rmsnorm(x, w):
    ms = jnp.mean(x * x, axis=-1, keepdims=True)
    return (x * lax.rsqrt(ms + EPS)) * w


def _causal_conv(x, prev, cw, cb):
    assert x.shape[0] > CONV_HALO
    acc = cb + cw[CONV_W - 1:CONV_W, :] * x
    row = lax.broadcasted_iota(jnp.int32, (CONV_HALO, x.shape[1]), 0)
    for back in range(1, CONV_W):
        tap = CONV_W - 1 - back
        rolled = pltpu.roll(x, back, axis=0)
        head = jnp.where(row < back, pltpu.roll(prev, back, axis=0), rolled[0:CONV_HALO, :])
        acc = acc + cw[tap:tap + 1, :] * jnp.concatenate([head, rolled[CONV_HALO:, :]], axis=0)
    return acc


def _even_in_kernel(t, chunk, x_ref, ltail_ref, nw_ref, w_ref, wdt_ref, wdte_ref, wdto_ref, lcw_ref, lcb_ref,
                    lxc_ref, lg_ref, z_ref, xbc_ref, dt_ref, dtp_ref, ltail_out_ref, lbuf):
    i = pl.program_id(1)

    @pl.when(i == 0)
    def _():
        lbuf[...] = ltail_ref[0]

    u = _rmsnorm(x_ref[0], nw_ref[...]).astype(BF16)
    c0, c1, c2, c3 = LRU_WIDTH, 2 * LRU_WIDTH, 2 * LRU_WIDTH + SSD_WIDTH, 2 * LRU_WIDTH + SSD_WIDTH + SSD_CONV_DIM
    strip = 512
    pending = None

    def finish(off, proj):
        cs = slice(off, off + strip)
        lxc_ref[0, :, cs] = _causal_conv(proj, lbuf[:, cs], lcw_ref[:, cs], lcb_ref[:, cs])
        lbuf[:, cs] = proj[t - CONV_HALO:t, :]

    for off in range(0, LRU_WIDTH, strip):
        proj = jnp.dot(u, w_ref[:, off:off + strip], preferred_element_type=F32)
        if pending is not None:
            finish(*pending)
        pending = (off, proj)
    xbc_ref[0] = jnp.dot(u, w_ref[:, c2:c3], preferred_element_type=F32)
    finish(*pending)
    lg_ref[0] = jnp.dot(u, w_ref[:, c0:c1], preferred_element_type=F32)
    z_ref[0] = jnp.dot(u, w_ref[:, c1:c2], preferred_element_type=F32)
    dt_ref[0] = jnp.dot(u, wdt_ref[...], preferred_element_type=F32)
    for c in range(t // chunk):
        uc = u[c * chunk:(c + 1) * chunk, :]
        even = lax.dot_general(wdte_ref[...], uc, _NT, preferred_element_type=F32)
        odd = lax.dot_general(wdto_ref[...], uc, _NT, preferred_element_type=F32)
        dtp_ref[c] = jnp.concatenate([even, odd], axis=1)

    @pl.when(i == pl.num_programs(1) - 1)
    def _():
        ltail_out_ref[0] = lbuf[...]


def _even_in(x3, ltail, p, tile, chunk):
    bn, seq, _ = x3.shape
    nt = seq // tile
    n_main = p["w_main"].shape[1]
    tile_spec = lambda c: pl.BlockSpec((1, tile, c), lambda b, i: (b, i, 0))
    shared = lambda c: pl.BlockSpec((1, CONV_HALO, c), lambda b, i: (0, 0, 0))
    per_b = lambda c: pl.BlockSpec((1, CONV_HALO, c), lambda b, i: (b, 0, 0))
    act = lambda c, dt=F32: jax.ShapeDtypeStruct((bn, seq, c), dt)
    return pl.pallas_call(
        functools.partial(_even_in_kernel, tile, chunk),
        grid=(bn, nt),
        in_specs=[tile_spec(D_MODEL), shared(LRU_WIDTH),
                  _const_spec((1, D_MODEL)), _const_spec((D_MODEL, n_main)), _const_spec((D_MODEL, LANES)),
                  _const_spec((SSD_PAIRS, D_MODEL)), _const_spec((SSD_PAIRS, D_MODEL)),
                  _const_spec((CONV_W, LRU_WIDTH)), _const_spec((1, LRU_WIDTH))],
        out_specs=[tile_spec(LRU_WIDTH), tile_spec(LRU_WIDTH), tile_spec(SSD_WIDTH), tile_spec(SSD_CONV_DIM),
                   tile_spec(LANES),
                   pl.BlockSpec((tile // chunk, SSD_PAIRS, 2 * chunk), lambda b, i: (b * nt + i, 0, 0)),
                   per_b(LRU_WIDTH)],
        out_shape=[act(LRU_WIDTH), act(LRU_WIDTH), act(SSD_WIDTH), act(SSD_CONV_DIM), act(LANES),
                   jax.ShapeDtypeStruct((bn * seq // chunk, SSD_PAIRS, 2 * chunk), F32),
                   jax.ShapeDtypeStruct((bn, CONV_HALO, LRU_WIDTH), F32)],
        scratch_shapes=[pltpu.VMEM((CONV_HALO, LRU_WIDTH), F32)],
        compiler_params=_params(2),
        name="even_in",
    )(x3, ltail, p["even_norm"], p["w_main"], p["w_dt"], p["w_dt_even"], p["w_dt_odd"],
      p["lru_conv_w"], p["lru_conv_b"])


def _lru_kernel(t, lx_ref, lg_ref, h0_ref, wa_ref, ba_ref, wx_ref, bx_ref, lam_ref,
                ya_ref, h_out_ref, abuf, bbuf, hbuf, hcar):
    i = pl.program_id(1)

    @pl.when(i == 0)
    def _():
        hcar[...] = h0_ref[0]

    lx = lx_ref[0]
    lxb = lx.astype(BF16)
    for g in range(LRU_BLOCKS):
        sl = slice(g * LRU_BLOCK, (g + 1) * LRU_BLOCK)
        xg = lxb[:, sl]
        tr = jnp.tanh(jnp.dot(xg, wa_ref[g], preferred_element_type=F32) + ba_ref[:, sl])
        ti = jnp.tanh(jnp.dot(xg, wx_ref[g], preferred_element_type=F32) + bx_ref[:, sl])
        half_c_sp = (-0.5 * RG_LRU_C) * _softplus(-lam_ref[:, sl])
        a = jnp.exp(half_c_sp * tr + half_c_sp)
        mult = jnp.sqrt(1.0 - a * a)
        abuf[:, sl] = a
        bbuf[:, sl] = (mult * (ti + 1.0)) * (0.5 * lx[:, sl])

    def step(row, h):
        h = abuf[pl.ds(row, 1), :] * h + bbuf[pl.ds(row, 1), :]
        hbuf[pl.ds(row, 1), :] = h
        return h

    h = lax.fori_loop(0, t, step, hcar[...], unroll=8)
    hcar[...] = h
    ya_ref[0] = (hbuf[...] * _silu(lg_ref[0])).astype(BF16)

    @pl.when(i == pl.num_programs(1) - 1)
    def _():
        h_out_ref[0] = h


def _lru(lxc, lg, h0, p, tile):
    bn, seq, _ = lxc.shape
    tile_spec = pl.BlockSpec((1, tile, LRU_WIDTH), lambda b, i: (b, i, 0))
    return pl.pallas_call(
        functools.partial(_lru_kernel, tile),
        grid=(bn, seq // tile),
        in_specs=[tile_spec, tile_spec, pl.BlockSpec((1, 1, LRU_WIDTH), lambda b, i: (0, 0, 0)),
                  _const_spec((LRU_BLOCKS, LRU_BLOCK, LRU_BLOCK)), _const_spec((1, LRU_WIDTH)),
                  _const_spec((LRU_BLOCKS, LRU_BLOCK, LRU_BLOCK)), _const_spec((1, LRU_WIDTH)),
                  _const_spec((1, LRU_WIDTH))],
        out_specs=[tile_spec, pl.BlockSpec((1, 1, LRU_WIDTH), lambda b, i: (b, 0, 0))],
        out_shape=[jax.ShapeDtypeStruct((bn, seq, LRU_WIDTH), BF16),
                   jax.ShapeDtypeStruct((bn, 1, LRU_WIDTH), F32)],
        scratch_shapes=[pltpu.VMEM((tile, LRU_WIDTH), F32), pltpu.VMEM((tile, LRU_WIDTH), F32),
                        pltpu.VMEM((tile, LRU_WIDTH), F32), pltpu.VMEM((1, LRU_WIDTH), F32)],
        compiler_params=_params(2),
        name="rg_lru",
    )(lxc, lg, h0, p["lru_w_a"], p["lru_b_a"], p["lru_w_x"], p["lru_b_x"], p["lru_lambda"])


def _ssd_kernel(t, q, xbc_ref, z_ref, dt_ref, dtp_ref, tail_ref, s0_ref, cw_ref, cb_ref, dtb_ref, dtbp_ref,
                alog_ref, alogp_ref, dexp_ref, nrm_ref, yb_ref, tail_out_ref, s_out_ref, ybuf, xsbuf, sbuf, tbuf):
    i = pl.program_id(1)

    @pl.when(i == 0)
    def _():
        sbuf[...] = s0_ref[0]
        tbuf[...] = tail_ref[0]

    a_row = -jnp.exp(alog_ref[...])
    a_pk = -jnp.exp(alogp_ref[...])
    rr = lax.broadcasted_iota(jnp.int32, (q, q), 0)
    cc = lax.broadcasted_iota(jnp.int32, (q, q), 1)
    tril = (cc <= rr).astype(F32)
    r2 = lax.broadcasted_iota(jnp.int32, (2 * q, 2 * q), 0)
    c2 = lax.broadcasted_iota(jnp.int32, (2 * q, 2 * q), 1)
    bd = ((r2 <= c2) & ((r2 < q) == (c2 < q))).astype(F32)
    lane_x = lax.broadcasted_iota(jnp.int32, (q, LANES), 1)
    first_x = lane_x < SSD_HEAD_DIM
    lane_m = lax.broadcasted_iota(jnp.int32, (q, 2 * q), 1)
    first_m = lane_m < q
    row_m = lax.broadcasted_iota(jnp.int32, (q, 2 * q), 0)
    causal_m = jnp.where(first_m, lane_m, lane_m - q) <= row_m

    n_par = min(2, t // q)

    def chunks(step, carry):
        ks = range(n_par)
        cidx = [step * n_par + k for k in ks]
        rows = [pl.ds(pl.multiple_of(cidx[k] * q, q), q) for k in ks]
        xc = []
        for k in ks:
            before = xbc_ref[0, pl.ds(pl.multiple_of(jnp.maximum(cidx[k] * q - CONV_HALO, 0), CONV_HALO),
                                      CONV_HALO), :]
            prev = jnp.where(cidx[k] == 0, tbuf[...], before)
            xc.append(_silu(_causal_conv(xbc_ref[0, rows[k], :], prev, cw_ref[...], cb_ref[...])))
            xsbuf[rows[k], :] = xc[k][:, 0:SSD_WIDTH]
        dt = [_softplus(dt_ref[0, rows[k], :] + dtb_ref[...]) for k in ks]
        dtp = [_softplus(dtp_ref[cidx[k]] + dtbp_ref[...]) for k in ks]
        acum = [jnp.dot(tril, dt[k] * a_row, precision=_EXACT, preferred_element_type=F32) for k in ks]
        acum_pk = [jnp.dot(dtp[k] * a_pk, bd, precision=_EXACT, preferred_element_type=F32) for k in ks]
        b_g, c_g = [[None] * SSD_GROUPS for _ in ks], [[None] * SSD_GROUPS for _ in ks]
        y_diag, wx, e_acum, e_tot = ([[None] * SSD_PAIRS for _ in ks] for _ in range(4))
        for g in range(SSD_GROUPS):
            b_lo = SSD_WIDTH + g * SSD_STATE
            c_lo = SSD_WIDTH + (SSD_GROUPS + g) * SSD_STATE
            cbb = []
            for k in ks:
                b_g[k][g] = xc[k][:, b_lo:b_lo + SSD_STATE].astype(BF16)
                c_g[k][g] = xc[k][:, c_lo:c_lo + SSD_STATE].astype(BF16)
                cbb.append(lax.dot_general(c_g[k][g], jnp.concatenate([b_g[k][g]] * 2, axis=0), _NT,
                                           preferred_element_type=F32))
            for pp in range(SSD_PAIRS_PER_GROUP):
                pr = g * SSD_PAIRS_PER_GROUP + pp
                h0, h1 = 2 * pr, 2 * pr + 1
                for k in ks:
                    acol_x = jnp.where(first_x, acum[k][:, h0:h0 + 1], acum[k][:, h1:h1 + 1])
                    dt_x = jnp.where(first_x, dt[k][:, h0:h0 + 1], dt[k][:, h1:h1 + 1])
                    if 2 * q == LANES:
                        acol_m = acol_x
                    else:
                        acol_m = jnp.where(first_m, acum[k][:, h0:h0 + 1], acum[k][:, h1:h1 + 1])
                    xdt = xc[k][:, pr * LANES:(pr + 1) * LANES] * dt_x
                    xdt_b = xdt.astype(BF16)
                    decay = jnp.where(causal_m, jnp.exp(acol_m - acum_pk[k][pr:pr + 1, :]), 0.0)
                    m = (cbb[k] * decay).astype(BF16)
                    zeros = jnp.zeros_like(xdt_b)
                    rhs = jnp.concatenate([jnp.where(first_x, xdt_b, zeros), jnp.where(first_x, zeros, xdt_b)],
                                          axis=0)
                    y_diag[k][pr] = jnp.dot(m, rhs, preferred_element_type=F32)
                    tot_x = acol_x[q - 1:q, :]
                    wx[k][pr] = (jnp.exp(tot_x - acol_x) * xdt).astype(BF16)
                    e_acum[k][pr] = jnp.exp(acol_x)
                    e_tot[k][pr] = jnp.exp(tot_x)
        for k in ks:
            for g in range(SSD_GROUPS):
                prs = slice(g * SSD_PAIRS_PER_GROUP, (g + 1) * SSD_PAIRS_PER_GROUP)
                cols = slice(g * SSD_GROUP_WIDTH, (g + 1) * SSD_GROUP_WIDTH)
                state = sbuf[g]
                y_off = jnp.dot(c_g[k][g], state.astype(BF16), preferred_element_type=F32)
                ybuf[rows[k], cols] = (jnp.concatenate(y_diag[k][prs], axis=1)
                                       + y_off * jnp.concatenate(e_acum[k][prs], axis=1))
                sbuf[g] = jnp.concatenate(e_tot[k][prs], axis=1) * state + lax.dot_general(
                    b_g[k][g], jnp.concatenate(wx[k][prs], axis=1), _TN, preferred_element_type=F32)
        return carry

    lax.fori_loop(0, t // (q * n_par), chunks, 0)

    y = ybuf[...] + xsbuf[...] * dexp_ref[...]
    tbuf[...] = xbc_ref[0, t - CONV_HALO:t, :]
    zz = z_ref[0]
    gt = y * _silu(zz)
    for g in range(SSD_GROUPS):
        cols = slice(g * SSD_GROUP_WIDTH, (g + 1) * SSD_GROUP_WIDTH)
        gg = gt[:, cols]
        ms = jnp.mean(gg * gg, axis=-1, keepdims=True)
        yb_ref[0, :, cols] = (gg * lax.rsqrt(ms + EPS) * nrm_ref[:, cols]).astype(BF16)

    @pl.when(i == pl.num_programs(1) - 1)
    def _():
        s_out_ref[0] = sbuf[...]
        tail_out_ref[0] = tbuf[...]


def _ssd(xbc, z, dt, dtp, tail, s0, p, dtbp, alogp, tile, chunk):
    bn, seq, _ = xbc.shape
    nt = seq // tile
    tile_spec = lambda c: pl.BlockSpec((1, tile, c), lambda b, i: (b, i, 0))
    state_shape = (SSD_GROUPS, SSD_STATE, SSD_GROUP_WIDTH)
    return pl.pallas_call(
        functools.partial(_ssd_kernel, tile, chunk),
        grid=(bn, nt),
        in_specs=[tile_spec(SSD_CONV_DIM), tile_spec(SSD_WIDTH), tile_spec(LANES),
                  pl.BlockSpec((tile // chunk, SSD_PAIRS, 2 * chunk), lambda b, i: (b * nt + i, 0, 0)),
                  pl.BlockSpec((1, CONV_HALO, SSD_CONV_DIM), lambda b, i: (0, 0, 0)),
                  pl.BlockSpec((1,) + state_shape, lambda b, i: (0, 0, 0, 0)),
                  _const_spec((CONV_W, SSD_CONV_DIM)), _const_spec((1, SSD_CONV_DIM)),
                  _const_spec((1, LANES)), _const_spec((SSD_PAIRS, 2 * chunk)),
                  _const_spec((1, LANES)), _const_spec((SSD_PAIRS, 2 * chunk)),
                  _const_spec((1, SSD_WIDTH)), _const_spec((1, SSD_WIDTH))],
        out_specs=[tile_spec(SSD_WIDTH), pl.BlockSpec((1, CONV_HALO, SSD_CONV_DIM), lambda b, i: (b, 0, 0)),
                   pl.BlockSpec((1,) + state_shape, lambda b, i: (b, 0, 0, 0))],
        out_shape=[jax.ShapeDtypeStruct((bn, seq, SSD_WIDTH), BF16),
                   jax.ShapeDtypeStruct((bn, CONV_HALO, SSD_CONV_DIM), F32),
                   jax.ShapeDtypeStruct((bn,) + state_shape, F32)],
        scratch_shapes=[pltpu.VMEM((tile, SSD_WIDTH), F32), pltpu.VMEM((tile, SSD_WIDTH), F32),
                        pltpu.VMEM(state_shape, F32), pltpu.VMEM((CONV_HALO, SSD_CONV_DIM), F32)],
        compiler_params=_params(2),
        name="ssd",
    )(xbc, z, dt, dtp, tail, s0, p["ssd_conv_w"], p["ssd_conv_b"], p["dt_bias_row"], dtbp, p["a_log_row"], alogp,
      p["d_expanded"], p["ssd_norm"])


def _mid_body(x_ref, ya_ref, yb_ref, wo_ref, nw_ref, wi_ref, h1_ref, q_ref, k_ref, v_ref, g_ref):
    h1 = (x_ref[0] + jnp.dot(ya_ref[0], wo_ref[0:LRU_WIDTH, :], preferred_element_type=F32)
          + jnp.dot(yb_ref[0], wo_ref[LRU_WIDTH:LRU_WIDTH + SSD_WIDTH, :], preferred_element_type=F32))
    h1_ref[0] = h1
    u = _rmsnorm(h1, nw_ref[...]).astype(BF16)
    w = D_MODEL
    q_ref[0] = (jnp.dot(u, wi_ref[:, 0:w], preferred_element_type=F32) * (SB_HEAD_DIM ** -0.5)).astype(BF16)
    k_ref[0] = jnp.dot(u, wi_ref[:, w:2 * w], preferred_element_type=F32).astype(BF16)
    v_ref[0] = jnp.dot(u, wi_ref[:, 2 * w:3 * w], preferred_element_type=F32).astype(BF16)
    g_ref[0] = jnp.dot(u, wi_ref[:, 3 * w:4 * w], preferred_element_type=F32)


def _mid_front_kernel(tile, x_ref, ya_ref, yb_ref, wo_ref, nw_ref, wi_ref, km_ref, vm_ref,
                      h1_ref, q_ref, k_ref, v_ref, g_ref):
    i = pl.program_id(1)

    @pl.when(i == 0)
    def _():
        for dst, src in ((k_ref, km_ref), (v_ref, vm_ref)):
            dst[0, 0:tile - N_META, :] = jnp.zeros((tile - N_META, D_MODEL), BF16)
            dst[0, tile - N_META:tile, :] = src[0]

    @pl.when(i > 0)
    def _():
        _mid_body(x_ref, ya_ref, yb_ref, wo_ref, nw_ref, wi_ref, h1_ref, q_ref, k_ref, v_ref, g_ref)


def _mid(x3, ya, yb, p, tile, meta_kv=None):
    bn, seq, _ = x3.shape
    nt = seq // tile
    act = lambda dt, rows=seq: jax.ShapeDtypeStruct((bn, rows, D_MODEL), dt)
    weights = [_const_spec((LRU_WIDTH + SSD_WIDTH, D_MODEL)), _const_spec((1, D_MODEL)),
               _const_spec((D_MODEL, 4 * D_MODEL))]
    args = [x3, ya, yb, p["even_w_out"], p["odd_norm"], p["odd_w_in"]]
    if meta_kv is None:
        tile_spec = pl.BlockSpec((1, tile, D_MODEL), lambda b, i: (b, i, 0))
        return pl.pallas_call(
            _mid_body,
            grid=(bn, nt),
            in_specs=[tile_spec] * 3 + weights,
            out_specs=[tile_spec] * 5,
            out_shape=[act(F32), act(BF16), act(BF16), act(BF16), act(F32)],
            compiler_params=_params(2),
            name="even_out_odd_in",
        )(*args)
    tile_spec = pl.BlockSpec((1, tile, D_MODEL), lambda b, i: (b, jnp.maximum(i - 1, 0), 0))
    kv_spec = pl.BlockSpec((1, tile, D_MODEL), lambda b, i: (b, i, 0))
    meta_spec = pl.BlockSpec((1, N_META, D_MODEL), lambda b, i: (0, 0, 0))
    return pl.pallas_call(
        functools.partial(_mid_front_kernel, tile),
        grid=(bn, nt + 1),
        in_specs=[tile_spec] * 3 + weights + [meta_spec] * 2,
        out_specs=[tile_spec, tile_spec, kv_spec, kv_spec, tile_spec],
        out_shape=[act(F32), act(BF16), act(BF16, tile + seq), act(BF16, tile + seq), act(F32)],
        compiler_params=_params(2),
        name="even_out_odd_in",
    )(*args, *meta_kv)


def _sb_blocks(qns, kbs, vbs, rs, keep):
    n = len(qns)
    tk = kbs[0].shape[0]
    jj = lax.broadcasted_iota(jnp.int32, (tk, tk), 0)
    ss = lax.broadcasted_iota(jnp.int32, (tk, tk), 1)
    later = (jj > ss).astype(BF16)
    zns = [lax.dot_general(qns[c], kbs[c], _NT, preferred_element_type=F32) for c in range(n)]
    lks, logbs, csums = [], [], []
    for c in range(n):
        zn = zns[c]
        lk = jnp.minimum(zn, 0.0) - jnp.log(1.0 + jnp.exp2(jnp.abs(zn) * (-LOG2_E)))
        logbs.append(lk - zn)
        lks.append(jnp.where(keep, lk, 0.0))
    for c in range(n):
        csums.append(jnp.dot(lks[c].astype(BF16), later, preferred_element_type=F32))
    res = []
    for c in range(n):
        logw = logbs[c] + csums[c]
        if rs is not None:
            logw = logw + rs[c]
        wgt = jnp.where(keep, jnp.exp(logw), 0.0)
        out = jnp.dot(wgt.astype(BF16), vbs[c], preferred_element_type=F32)
        total = jnp.sum(lks[c], axis=-1, keepdims=True)
        res.append((out, total if rs is None else rs[c] + total))
    return res


def _attn_kernel(tq, tk, n_tiles, pad, q_ref, k_ref, v_ref, g_ref, o_ref, acc_ref, r_ref):
    i = pl.program_id(2)
    hpt = HEADS_PER_LANE_TILE
    lane = lax.broadcasted_iota(jnp.int32, (tq, LANES), 1)
    head_lanes = [(lane >= h * SB_HEAD_DIM) & (lane < (h + 1) * SB_HEAD_DIM) for h in range(hpt)]
    cols = [slice(c * LANES, (c + 1) * LANES) for c in range(n_tiles)]
    qn = []
    for c in range(n_tiles):
        q = -q_ref[0, :, cols[c]]
        qn.append(jnp.concatenate([jnp.where(m, q, jnp.zeros_like(q)) for m in head_lanes], axis=0))
    q_row = lax.broadcasted_iota(jnp.int32, (tq, tk), 0)
    k_col = lax.broadcasted_iota(jnp.int32, (tq, tk), 1)

    def sweep(start, rs, causal):
        exists = k_col >= (pad - N_META) - start
        keep = exists if causal is None else causal & exists
        keep = jnp.concatenate([keep] * hpt, axis=0)
        rows = pl.ds(pl.multiple_of(start, tq), tk)
        return _sb_blocks(qn, [k_ref[0, rows, cols[c]] for c in range(n_tiles)],
                          [v_ref[0, rows, cols[c]] for c in range(n_tiles)], rs, keep)

    def row_max(res):
        rm = jnp.max(res[0][1])
        for c in range(1, n_tiles):
            rm = jnp.maximum(rm, jnp.max(res[c][1]))
        return rm

    start0 = pad + i * tq - (tk - tq)
    res = sweep(start0, None, k_col < q_row + (tk - tq))
    for c in range(n_tiles):
        acc_ref[c] = res[c][0]
        r_ref[c] = res[c][1]

    def cond(carry):
        start, rm = carry
        return (start + tk > pad - N_META) & (rm > F32_EXP_ZERO_BELOW)

    def body(carry):
        start, _ = carry
        res = sweep(start, [r_ref[c] for c in range(n_tiles)], None)
        for c in range(n_tiles):
            acc_ref[c] += res[c][0]
            r_ref[c] = res[c][1]
        return start - tk, row_max(res)

    lax.while_loop(cond, body, (start0 - tk, row_max(res)))

    for c in range(n_tiles):
        o = acc_ref[c, 0:tq, :]
        for h in range(1, hpt):
            o = jnp.where(head_lanes[h], acc_ref[c, h * tq:(h + 1) * tq, :], o)
        g = g_ref[0, :, cols[c]]
        o_ref[0, :, cols[c]] = (o * _silu(g)).astype(BF16)


def _attn(q, k, v, g, tq, tk, n_tiles, pad):
    bn, seq, width = q.shape
    assert tk % tq == 0 and pad % tq == 0 and pad >= tk and k.shape[1] == pad + seq
    bw = n_tiles * LANES
    grid = (bn, width // bw, seq // tq)
    q_spec = pl.BlockSpec((1, tq, bw), lambda b, p, i: (b, i, p))
    kv_spec = pl.BlockSpec((1, pad + seq, bw), lambda b, p, i: (b, 0, p))
    return pl.pallas_call(
        functools.partial(_attn_kernel, tq, tk, n_tiles, pad),
        grid=grid,
        in_specs=[q_spec, kv_spec, kv_spec, q_spec],
        out_specs=q_spec,
        out_shape=jax.ShapeDtypeStruct((bn, seq, width), BF16),
        scratch_shapes=[pltpu.VMEM((n_tiles, HEADS_PER_LANE_TILE * tq, LANES), F32),
                        pltpu.VMEM((n_tiles, HEADS_PER_LANE_TILE * tq, 1), F32)],
        compiler_params=_params(3),
        name="stick_breaking",
    )(q, k, v, g)


def _final_kernel(h1_ref, o_ref, wo_ref, fw_ref, out_ref):
    h2 = h1_ref[...] + jnp.dot(o_ref[...], wo_ref[...], preferred_element_type=F32)
    out_ref[...] = _rmsnorm(h2, fw_ref[...])


def _final(h1, o, wo, fw, tile):
    rows = h1.shape[0]
    row_spec = pl.BlockSpec((tile, D_MODEL), lambda i: (i, 0))
    return pl.pallas_call(
        _final_kernel,
        grid=(rows // tile,),
        in_specs=[row_spec, row_spec, _const_spec((D_MODEL, D_MODEL)), _const_spec((1, D_MODEL))],
        out_specs=row_spec,
        out_shape=jax.ShapeDtypeStruct((rows, D_MODEL), F32),
        compiler_params=_params(1),
        name="odd_out_final_norm",
    )(h1, o, wo, fw)


def _pair_lanes(vec, chunk):
    return jnp.repeat(vec.astype(F32).reshape(SSD_PAIRS, HEADS_PER_LANE_TILE), chunk, axis=1)


def _layer0_and_qkv(x3, carry, p, row_tile, lru_tile, ssd_tile, chunk, meta_kv=None):
    lxc, lg, z, xbc, dt, dtp, lru_tail = _even_in(x3, carry["lru_tail"], p, row_tile, chunk)
    ya, lru_h = _lru(lxc, lg, carry["lru_h"], p, lru_tile)
    yb, ssd_tail, ssd_s = _ssd(xbc, z, dt, dtp, carry["ssd_tail"], carry["ssd_s"], p,
                               _pair_lanes(p["dt_bias"], chunk), _pair_lanes(p["a_log"], chunk), ssd_tile, chunk)
    h1, q, k, v, g = _mid(x3, ya, yb, p, row_tile, meta_kv)
    new_carry = {"lru_tail": lru_tail, "lru_h": lru_h, "ssd_tail": ssd_tail, "ssd_s": ssd_s}
    return h1, q, k, v, g, new_carry


def kernel(x, meta, even_norm, even_w_in, lru_conv_w, lru_conv_b, lru_w_a, lru_b_a, lru_w_x, lru_b_x, lru_lambda,
           ssd_conv_w, ssd_conv_b, ssd_dt_bias, ssd_a_log, ssd_d, ssd_norm, even_w_out, odd_norm, odd_w_in,
           odd_w_out, final_norm):
    bsz, seq, _ = x.shape
    n_main = 2 * LRU_WIDTH + SSD_WIDTH + SSD_CONV_DIM
    w_in = even_w_in[0]
    w_dt = w_in[:, n_main:]

    def lane_row(vec):
        return jnp.pad(vec.astype(F32), (0, LANES - vec.shape[0])).reshape(1, LANES)

    p = {
        "even_norm": even_norm[0].reshape(1, -1),
        "w_main": w_in[:, :n_main].astype(BF16),
        "w_dt": jnp.pad(w_dt, ((0, 0), (0, LANES - SSD_HEADS))).astype(BF16),
        "w_dt_even": w_dt[:, 0::2].T.astype(BF16), "w_dt_odd": w_dt[:, 1::2].T.astype(BF16),
        "lru_conv_w": lru_conv_w[0], "lru_conv_b": lru_conv_b[0].reshape(1, -1),
        "lru_w_a": (0.5 * lru_w_a[0]).astype(BF16), "lru_b_a": 0.5 * lru_b_a[0].reshape(1, -1),
        "lru_w_x": (0.5 * lru_w_x[0]).astype(BF16), "lru_b_x": 0.5 * lru_b_x[0].reshape(1, -1),
        "lru_lambda": lru_lambda[0].reshape(1, -1),
        "ssd_conv_w": ssd_conv_w[0], "ssd_conv_b": ssd_conv_b[0].reshape(1, -1),
        "dt_bias": ssd_dt_bias[0], "a_log": ssd_a_log[0],
        "dt_bias_row": lane_row(ssd_dt_bias[0]), "a_log_row": lane_row(ssd_a_log[0]),
        "d_expanded": jnp.repeat(ssd_d[0], SSD_HEAD_DIM).reshape(1, -1),
        "ssd_norm": ssd_norm[0].reshape(1, -1),
        "even_w_out": even_w_out[0].astype(BF16),
        "odd_norm": odd_norm[0].reshape(1, -1),
        "odd_w_in": odd_w_in[0].astype(BF16),
    }
    zero_carry = {
        "lru_tail": jnp.zeros((1, CONV_HALO, LRU_WIDTH), F32), "lru_h": jnp.zeros((1, 1, LRU_WIDTH), F32),
        "ssd_tail": jnp.zeros((1, CONV_HALO, SSD_CONV_DIM), F32),
        "ssd_s": jnp.zeros((1, SSD_GROUPS, SSD_STATE, SSD_GROUP_WIDTH), F32),
    }
    _, _, km, vm, _, carry = _layer0_and_qkv(meta.astype(x.dtype)[None], zero_carry, p,
                                             row_tile=N_META, lru_tile=N_META, ssd_tile=N_META, chunk=N_META)
    row_tile = min(512, seq)
    h1, q, k, v, g, _ = _layer0_and_qkv(x, carry, p, row_tile=row_tile, lru_tile=min(512, seq),
                                        ssd_tile=min(512, seq), chunk=64, meta_kv=(km, vm))
    o = _attn(q, k, v, g, tq=64, tk=256, n_tiles=8, pad=row_tile)
    out = _final(h1.reshape(bsz * seq, D_MODEL), o.reshape(bsz * seq, D_MODEL), odd_w_out[0].astype(BF16),
                 final_norm.reshape(1, -1), row_tile)
    return out.reshape(bsz, seq, D_MODEL)
```

```python
import functools

import jax
import jax.numpy as jnp
from jax import lax
from jax.experimental import pallas as pl
from jax.experimental.pallas import tpu as pltpu

F32 = jnp.float32
BF16 = jnp.bfloat16

D_MODEL = 1024
N_META = 16
EPS = 1e-6
CONV_W = 4
LRU_WIDTH = 1024
LRU_BLOCKS = 4
LRU_BLOCK = LRU_WIDTH // LRU_BLOCKS
RG_LRU_C = 8.0
SSD_WIDTH = 1024
SSD_HEAD_DIM = 64
SSD_HEADS = 16
SSD_GROUPS = 2
SSD_HPG = SSD_HEADS // SSD_GROUPS
SSD_STATE = 128
SSD_CONV_DIM = SSD_WIDTH + 2 * SSD_GROUPS * SSD_STATE
SSD_GROUP_WIDTH = SSD_WIDTH // SSD_GROUPS
SB_HEADS = 16
SB_HEAD_DIM = 64
LANES = 128
HEADS_PER_LANE_TILE = LANES // SB_HEAD_DIM
SSD_PAIRS = SSD_HEADS // HEADS_PER_LANE_TILE
SSD_PAIRS_PER_GROUP = SSD_PAIRS // SSD_GROUPS
CONV_HALO = 8
F32_EXP_ZERO_BELOW = -104.0
LOG2_E = 1.4426950408889634
VMEM_LIMIT = 56 * 1024 * 1024

_NT = (((1,), (1,)), ((), ()))
_TN = (((0,), (0,)), ((), ()))
_EXACT = lax.Precision.HIGHEST


def _params(n_grid):
    return pltpu.CompilerParams(dimension_semantics=("arbitrary",) * n_grid,
                                vmem_limit_bytes=VMEM_LIMIT)


def _const_spec(shape):
    zeros = (0,) * len(shape)
    return pl.BlockSpec(shape, lambda *_: zeros)


def _silu(x):
    half = 0.5 * x
    return half * (jnp.tanh(half) + 1.0)


def _softplus(x):
    return jnp.maximum(x, 0.0) + jnp.log(1.0 + jnp.exp(-jnp.abs(x)))


def _rmsnorm(x, w):
    ms = jnp.mean(x * x, axis=-1, keepdims=True)
    return (x * lax.rsqrt(ms + EPS)) * w


def _causal_conv(x, prev, cw, cb):
    assert x.shape[0] > CONV_HALO
    acc = cb + cw[CONV_W - 1:CONV_W, :] * x
    row = lax.broadcasted_iota(jnp.int32, (CONV_HALO, x.shape[1]), 0)
    for back in range(1, CONV_W):
        tap = CONV_W - 1 - back
        rolled = pltpu.roll(x, back, axis=0)
        head = jnp.where(row < back, pltpu.roll(prev, back, axis=0), rolled[0:CONV_HALO, :])
        acc = acc + cw[tap:tap + 1, :] * jnp.concatenate([head, rolled[CONV_HALO:, :]], axis=0)
    return acc


def _even_in_kernel(t, chunk, x_ref, ltail_ref, nw_ref, w_ref, wdt_ref, wdte_ref, wdto_ref, lcw_ref, lcb_ref,
                    lxc_ref, lg_ref, z_ref, xbc_ref, dt_ref, dtp_ref, ltail_out_ref, lbuf):
    i = pl.program_id(1)

    @pl.when(i == 0)
    def _():
        lbuf[...] = ltail_ref[0]

    u = _rmsnorm(x_ref[0], nw_ref[...]).astype(BF16)
    c0, c1, c2, c3 = LRU_WIDTH, 2 * LRU_WIDTH, 2 * LRU_WIDTH + SSD_WIDTH, 2 * LRU_WIDTH + SSD_WIDTH + SSD_CONV_DIM
    strip = 512
    pending = None

    def finish(off, proj):
        cs = slice(off, off + strip)
        lxc_ref[0, :, cs] = _causal_conv(proj, lbuf[:, cs], lcw_ref[:, cs], lcb_ref[:, cs])
        lbuf[:, cs] = proj[t - CONV_HALO:t, :]

    for off in range(0, LRU_WIDTH, strip):
        proj = jnp.dot(u, w_ref[:, off:off + strip], preferred_element_type=F32)
        if pending is not None:
            finish(*pending)
        pending = (off, proj)
    xbc_ref[0] = jnp.dot(u, w_ref[:, c2:c3], preferred_element_type=F32)
    finish(*pending)
    lg_ref[0] = jnp.dot(u, w_ref[:, c0:c1], preferred_element_type=F32)
    z_ref[0] = jnp.dot(u, w_ref[:, c1:c2], preferred_element_type=F32)
    dt_ref[0] = jnp.dot(u, wdt_ref[...], preferred_element_type=F32)
    for c in range(t // chunk):
        uc = u[c * chunk:(c + 1) * chunk, :]
        even = lax.dot_general(wdte_ref[...], uc, _NT, preferred_element_type=F32)
        odd = lax.dot_general(wdto_ref[...], uc, _NT, preferred_element_type=F32)
        dtp_ref[c] = jnp.concatenate([even, odd], axis=1)

    @pl.when(i == pl.num_programs(1) - 1)
    def _():
        ltail_out_ref[0] = lbuf[...]


def _even_in(x3, ltail, p, tile, chunk):
    bn, seq, _ = x3.shape
    nt = seq // tile
    n_main = p["w_main"].shape[1]
    tile_spec = lambda c: pl.BlockSpec((1, tile, c), lambda b, i: (b, i, 0))
    shared = lambda c: pl.BlockSpec((1, CONV_HALO, c), lambda b, i: (0, 0, 0))
    per_b = lambda c: pl.BlockSpec((1, CONV_HALO, c), lambda b, i: (b, 0, 0))
    act = lambda c, dt=F32: jax.ShapeDtypeStruct((bn, seq, c), dt)
    return pl.pallas_call(
        functools.partial(_even_in_kernel, tile, chunk),
        grid=(bn, nt),
        in_specs=[tile_spec(D_MODEL), shared(LRU_WIDTH),
                  _const_spec((1, D_MODEL)), _const_spec((D_MODEL, n_main)), _const_spec((D_MODEL, LANES)),
                  _const_spec((SSD_PAIRS, D_MODEL)), _const_spec((SSD_PAIRS, D_MODEL)),
                  _const_spec((CONV_W, LRU_WIDTH)), _const_spec((1, LRU_WIDTH))],
        out_specs=[tile_spec(LRU_WIDTH), tile_spec(LRU_WIDTH), tile_spec(SSD_WIDTH), tile_spec(SSD_CONV_DIM),
                   tile_spec(LANES),
                   pl.BlockSpec((tile // chunk, SSD_PAIRS, 2 * chunk), lambda b, i: (b * nt + i, 0, 0)),
                   per_b(LRU_WIDTH)],
        out_shape=[act(LRU_WIDTH), act(LRU_WIDTH), act(SSD_WIDTH), act(SSD_CONV_DIM), act(LANES),
                   jax.ShapeDtypeStruct((bn * seq // chunk, SSD_PAIRS, 2 * chunk), F32),
                   jax.ShapeDtypeStruct((bn, CONV_HALO, LRU_WIDTH), F32)],
        scratch_shapes=[pltpu.VMEM((CONV_HALO, LRU_WIDTH), F32)],
        compiler_params=_params(2),
        name="even_in",
    )(x3, ltail, p["even_norm"], p["w_main"], p["w_dt"], p["w_dt_even"], p["w_dt_odd"],
      p["lru_conv_w"], p["lru_conv_b"])


def _lru_kernel(t, lx_ref, lg_ref, h0_ref, wa_ref, ba_ref, wx_ref, bx_ref, lam_ref,
                ya_ref, h_out_ref, abuf, bbuf, hbuf, hcar):
    i = pl.program_id(1)

    @pl.when(i == 0)
    def _():
        hcar[...] = h0_ref[0]

    lx = lx_ref[0]
    lxb = lx.astype(BF16)
    for g in range(LRU_BLOCKS):
        sl = slice(g * LRU_BLOCK, (g + 1) * LRU_BLOCK)
        xg = lxb[:, sl]
        tr = jnp.tanh(jnp.dot(xg, wa_ref[g], preferred_element_type=F32) + ba_ref[:, sl])
        ti = jnp.tanh(jnp.dot(xg, wx_ref[g], preferred_element_type=F32) + bx_ref[:, sl])
        half_c_sp = (-0.5 * RG_LRU_C) * _softplus(-lam_ref[:, sl])
        a = jnp.exp(half_c_sp * tr + half_c_sp)
        mult = jnp.sqrt(1.0 - a * a)
        abuf[:, sl] = a
        bbuf[:, sl] = (mult * (ti + 1.0)) * (0.5 * lx[:, sl])

    def step(row, h):
        h = abuf[pl.ds(row, 1), :] * h + bbuf[pl.ds(row, 1), :]
        hbuf[pl.ds(row, 1), :] = h
        return h

    h = lax.fori_loop(0, t, step, hcar[...], unroll=8)
    hcar[...] = h
    ya_ref[0] = (hbuf[...] * _silu(lg_ref[0])).astype(BF16)

    @pl.when(i == pl.num_programs(1) - 1)
    def _():
        h_out_ref[0] = h


def _lru(lxc, lg, h0, p, tile):
    bn, seq, _ = lxc.shape
    tile_spec = pl.BlockSpec((1, tile, LRU_WIDTH), lambda b, i: (b, i, 0))
    return pl.pallas_call(
        functools.partial(_lru_kernel, tile),
        grid=(bn, seq // tile),
        in_specs=[tile_spec, tile_spec, pl.BlockSpec((1, 1, LRU_WIDTH), lambda b, i: (0, 0, 0)),
                  _const_spec((LRU_BLOCKS, LRU_BLOCK, LRU_BLOCK)), _const_spec((1, LRU_WIDTH)),
                  _const_spec((LRU_BLOCKS, LRU_BLOCK, LRU_BLOCK)), _const_spec((1, LRU_WIDTH)),
                  _const_spec((1, LRU_WIDTH))],
        out_specs=[tile_spec, pl.BlockSpec((1, 1, LRU_WIDTH), lambda b, i: (b, 0, 0))],
        out_shape=[jax.ShapeDtypeStruct((bn, seq, LRU_WIDTH), BF16),
                   jax.ShapeDtypeStruct((bn, 1, LRU_WIDTH), F32)],
        scratch_shapes=[pltpu.VMEM((tile, LRU_WIDTH), F32), pltpu.VMEM((tile, LRU_WIDTH), F32),
                        pltpu.VMEM((tile, LRU_WIDTH), F32), pltpu.VMEM((1, LRU_WIDTH), F32)],
        compiler_params=_params(2),
        name="rg_lru",
    )(lxc, lg, h0, p["lru_w_a"], p["lru_b_a"], p["lru_w_x"], p["lru_b_x"], p["lru_lambda"])


def _ssd_kernel(t, q, xbc_ref, z_ref, dt_ref, dtp_ref, tail_ref, s0_ref, cw_ref, cb_ref, dtb_ref, dtbp_ref,
                alog_ref, alogp_ref, dexp_ref, nrm_ref, yb_ref, tail_out_ref, s_out_ref, ybuf, xsbuf, sbuf, tbuf):
    i = pl.program_id(1)

    @pl.when(i == 0)
    def _():
        sbuf[...] = s0_ref[0]
        tbuf[...] = tail_ref[0]

    a_row = -jnp.exp(alog_ref[...])
    a_pk = -jnp.exp(alogp_ref[...])
    rr = lax.broadcasted_iota(jnp.int32, (q, q), 0)
    cc = lax.broadcasted_iota(jnp.int32, (q, q), 1)
    tril = (cc <= rr).astype(F32)
    r2 = lax.broadcasted_iota(jnp.int32, (2 * q, 2 * q), 0)
    c2 = lax.broadcasted_iota(jnp.int32, (2 * q, 2 * q), 1)
    bd = ((r2 <= c2) & ((r2 < q) == (c2 < q))).astype(F32)
    lane_x = lax.broadcasted_iota(jnp.int32, (q, LANES), 1)
    first_x = lane_x < SSD_HEAD_DIM
    lane_m = lax.broadcasted_iota(jnp.int32, (q, 2 * q), 1)
    first_m = lane_m < q
    row_m = lax.broadcasted_iota(jnp.int32, (q, 2 * q), 0)
    causal_m = jnp.where(first_m, lane_m, lane_m - q) <= row_m

    n_par = min(2, t // q)

    def chunks(step, carry):
        ks = range(n_par)
        cidx = [step * n_par + k for k in ks]
        rows = [pl.ds(pl.multiple_of(cidx[k] * q, q), q) for k in ks]
        xc = []
        for k in ks:
            before = xbc_ref[0, pl.ds(pl.multiple_of(jnp.maximum(cidx[k] * q - CONV_HALO, 0), CONV_HALO),
                                      CONV_HALO), :]
            prev = jnp.where(cidx[k] == 0, tbuf[...], before)
            xc.append(_silu(_causal_conv(xbc_ref[0, rows[k], :], prev, cw_ref[...], cb_ref[...])))
            xsbuf[rows[k], :] = xc[k][:, 0:SSD_WIDTH]
        dt = [_softplus(dt_ref[0, rows[k], :] + dtb_ref[...]) for k in ks]
        dtp = [_softplus(dtp_ref[cidx[k]] + dtbp_ref[...]) for k in ks]
        acum = [jnp.dot(tril, dt[k] * a_row, precision=_EXACT, preferred_element_type=F32) for k in ks]
        acum_pk = [jnp.dot(dtp[k] * a_pk, bd, precision=_EXACT, preferred_element_type=F32) for k in ks]
        b_g, c_g = [[None] * SSD_GROUPS for _ in ks], [[None] * SSD_GROUPS for _ in ks]
        y_diag, wx, e_acum, e_tot = ([[None] * SSD_PAIRS for _ in ks] for _ in range(4))
        for g in range(SSD_GROUPS):
            b_lo = SSD_WIDTH + g * SSD_STATE
            c_lo = SSD_WIDTH + (SSD_GROUPS + g) * SSD_STATE
            cbb = []
            for k in ks:
                b_g[k][g] = xc[k][:, b_lo:b_lo + SSD_STATE].astype(BF16)
                c_g[k][g] = xc[k][:, c_lo:c_lo + SSD_STATE].astype(BF16)
                cbb.append(lax.dot_general(c_g[k][g], jnp.concatenate([b_g[k][g]] * 2, axis=0), _NT,
                                           preferred_element_type=F32))
            for pp in range(SSD_PAIRS_PER_GROUP):
                pr = g * SSD_PAIRS_PER_GROUP + pp
                h0, h1 = 2 * pr, 2 * pr + 1
                for k in ks:
                    acol_x = jnp.where(first_x, acum[k][:, h0:h0 + 1], acum[k][:, h1:h1 + 1])
                    dt_x = jnp.where(first_x, dt[k][:, h0:h0 + 1], dt[k][:, h1:h1 + 1])
                    if 2 * q == LANES:
                        acol_m = acol_x
                    else:
                        acol_m = jnp.where(first_m, acum[k][:, h0:h0 + 1], acum[k][:, h1:h1 + 1])
                    xdt = xc[k][:, pr * LANES:(pr + 1) * LANES] * dt_x
                    xdt_b = xdt.astype(BF16)
                    decay = jnp.where(causal_m, jnp.exp(acol_m - acum_pk[k][pr:pr + 1, :]), 0.0)
                    m = (cbb[k] * decay).astype(BF16)
                    zeros = jnp.zeros_like(xdt_b)
                    rhs = jnp.concatenate([jnp.where(first_x, xdt_b, zeros), jnp.where(first_x, zeros, xdt_b)],
                                          axis=0)
                    y_diag[k][pr] = jnp.dot(m, rhs, preferred_element_type=F32)
                    tot_x = acol_x[q - 1:q, :]
                    wx[k][pr] = (jnp.exp(tot_x - acol_x) * xdt).astype(BF16)
                    e_acum[k][pr] = jnp.exp(acol_x)
                    e_tot[k][pr] = jnp.exp(tot_x)
        for k in ks:
            for g in range(SSD_GROUPS):
                prs = slice(g * SSD_PAIRS_PER_GROUP, (g + 1) * SSD_PAIRS_PER_GROUP)
                cols = slice(g * SSD_GROUP_WIDTH, (g + 1) * SSD_GROUP_WIDTH)
                state = sbuf[g]
                y_off = jnp.dot(c_g[k][g], state.astype(BF16), preferred_element_type=F32)
                ybuf[rows[k], cols] = (jnp.concatenate(y_diag[k][prs], axis=1)
                                       + y_off * jnp.concatenate(e_acum[k][prs], axis=1))
                sbuf[g] = jnp.concatenate(e_tot[k][prs], axis=1) * state + lax.dot_general(
                    b_g[k][g], jnp.concatenate(wx[k][prs], axis=1), _TN, preferred_element_type=F32)
        return carry

    lax.fori_loop(0, t // (q * n_par), chunks, 0)

    y = ybuf[...] + xsbuf[...] * dexp_ref[...]
    tbuf[...] = xbc_ref[0, t - CONV_HALO:t, :]
    zz = z_ref[0]
    gt = y * _silu(zz)
    for g in range(SSD_GROUPS):
        cols = slice(g * SSD_GROUP_WIDTH, (g + 1) * SSD_GROUP_WIDTH)
        gg = gt[:, cols]
        ms = jnp.mean(gg * gg, axis=-1, keepdims=True)
        yb_ref[0, :, cols] = (gg * lax.rsqrt(ms + EPS) * nrm_ref[:, cols]).astype(BF16)

    @pl.when(i == pl.num_programs(1) - 1)
    def _():
        s_out_ref[0] = sbuf[...]
        tail_out_ref[0] = tbuf[...]


def _ssd(xbc, z, dt, dtp, tail, s0, p, dtbp, alogp, tile, chunk):
    bn, seq, _ = xbc.shape
    nt = seq // tile
    tile_spec = lambda c: pl.BlockSpec((1, tile, c), lambda b, i: (b, i, 0))
    state_shape = (SSD_GROUPS, SSD_STATE, SSD_GROUP_WIDTH)
    return pl.pallas_call(
        functools.partial(_ssd_kernel, tile, chunk),
        grid=(bn, nt),
        in_specs=[tile_spec(SSD_CONV_DIM), tile_spec(SSD_WIDTH), tile_spec(LANES),
                  pl.BlockSpec((tile // chunk, SSD_PAIRS, 2 * chunk), lambda b, i: (b * nt + i, 0, 0)),
                  pl.BlockSpec((1, CONV_HALO, SSD_CONV_DIM), lambda b, i: (0, 0, 0)),
                  pl.BlockSpec((1,) + state_shape, lambda b, i: (0, 0, 0, 0)),
                  _const_spec((CONV_W, SSD_CONV_DIM)), _const_spec((1, SSD_CONV_DIM)),
                  _const_spec((1, LANES)), _const_spec((SSD_PAIRS, 2 * chunk)),
                  _const_spec((1, LANES)), _const_spec((SSD_PAIRS, 2 * chunk)),
                  _const_spec((1, SSD_WIDTH)), _const_spec((1, SSD_WIDTH))],
        out_specs=[tile_spec(SSD_WIDTH), pl.BlockSpec((1, CONV_HALO, SSD_CONV_DIM), lambda b, i: (b, 0, 0)),
                   pl.BlockSpec((1,) + state_shape, lambda b, i: (b, 0, 0, 0))],
        out_shape=[jax.ShapeDtypeStruct((bn, seq, SSD_WIDTH), BF16),
                   jax.ShapeDtypeStruct((bn, CONV_HALO, SSD_CONV_DIM), F32),
                   jax.ShapeDtypeStruct((bn,) + state_shape, F32)],
        scratch_shapes=[pltpu.VMEM((tile, SSD_WIDTH), F32), pltpu.VMEM((tile, SSD_WIDTH), F32),
                        pltpu.VMEM(state_shape, F32), pltpu.VMEM((CONV_HALO, SSD_CONV_DIM), F32)],
        compiler_params=_params(2),
        name="ssd",
    )(xbc, z, dt, dtp, tail, s0, p["ssd_conv_w"], p["ssd_conv_b"], p["dt_bias_row"], dtbp, p["a_log_row"], alogp,
      p["d_expanded"], p["ssd_norm"])


def _mid_body(x_ref, ya_ref, yb_ref, wo_ref, nw_ref, wi_ref, h1_ref, q_ref, k_ref, v_ref, g_ref):
    h1 = (x_ref[0] + jnp.dot(ya_ref[0], wo_ref[0:LRU_WIDTH, :], preferred_element_type=F32)
          + jnp.dot(yb_ref[0], wo_ref[LRU_WIDTH:LRU_WIDTH + SSD_WIDTH, :], preferred_element_type=F32))
    h1_ref[0] = h1
    u = _rmsnorm(h1, nw_ref[...]).astype(BF16)
    w = D_MODEL
    q_ref[0] = (jnp.dot(u, wi_ref[:, 0:w], preferred_element_type=F32) * (SB_HEAD_DIM ** -0.5)).astype(BF16)
    k_ref[0] = jnp.dot(u, wi_ref[:, w:2 * w], preferred_element_type=F32).astype(BF16)
    v_ref[0] = jnp.dot(u, wi_ref[:, 2 * w:3 * w], preferred_element_type=F32).astype(BF16)
    g_ref[0] = jnp.dot(u, wi_ref[:, 3 * w:4 * w], preferred_element_type=F32)


def _mid_front_kernel(tile, x_ref, ya_ref, yb_ref, wo_ref, nw_ref, wi_ref, km_ref, vm_ref,
                      h1_ref, q_ref, k_ref, v_ref, g_ref):
    i = pl.program_id(1)

    @pl.when(i == 0)
    def _():
        for dst, src in ((k_ref, km_ref), (v_ref, vm_ref)):
            dst[0, 0:tile - N_META, :] = jnp.zeros((tile - N_META, D_MODEL), BF16)
            dst[0, tile - N_META:tile, :] = src[0]

    @pl.when(i > 0)
    def _():
        _mid_body(x_ref, ya_ref, yb_ref, wo_ref, nw_ref, wi_ref, h1_ref, q_ref, k_ref, v_ref, g_ref)


def _mid(x3, ya, yb, p, tile, meta_kv=None):
    bn, seq, _ = x3.shape
    nt = seq // tile
    act = lambda dt, rows=seq: jax.ShapeDtypeStruct((bn, rows, D_MODEL), dt)
    weights = [_const_spec((LRU_WIDTH + SSD_WIDTH, D_MODEL)), _const_spec((1, D_MODEL)),
               _const_spec((D_MODEL, 4 * D_MODEL))]
    args = [x3, ya, yb, p["even_w_out"], p["odd_norm"], p["odd_w_in"]]
    if meta_kv is None:
        tile_spec = pl.BlockSpec((1, tile, D_MODEL), lambda b, i: (b, i, 0))
        return pl.pallas_call(
            _mid_body,
            grid=(bn, nt),
            in_specs=[tile_spec] * 3 + weights,
            out_specs=[tile_spec] * 5,
            out_shape=[act(F32), act(BF16), act(BF16), act(BF16), act(F32)],
            compiler_params=_params(2),
            name="even_out_odd_in",
        )(*args)
    tile_spec = pl.BlockSpec((1, tile, D_MODEL), lambda b, i: (b, jnp.maximum(i - 1, 0), 0))
    kv_spec = pl.BlockSpec((1, tile, D_MODEL), lambda b, i: (b, i, 0))
    meta_spec = pl.BlockSpec((1, N_META, D_MODEL), lambda b, i: (0, 0, 0))
    return pl.pallas_call(
        functools.partial(_mid_front_kernel, tile),
        grid=(bn, nt + 1),
        in_specs=[tile_spec] * 3 + weights + [meta_spec] * 2,
        out_specs=[tile_spec, tile_spec, kv_spec, kv_spec, tile_spec],
        out_shape=[act(F32), act(BF16), act(BF16, tile + seq), act(BF16, tile + seq), act(F32)],
        compiler_params=_params(2),
        name="even_out_odd_in",
    )(*args, *meta_kv)


def _sb_blocks(qns, kbs, vbs, rs, keep):
    n = len(qns)
    tk = kbs[0].shape[0]
    jj = lax.broadcasted_iota(jnp.int32, (tk, tk), 0)
    ss = lax.broadcasted_iota(jnp.int32, (tk, tk), 1)
    later = (jj > ss).astype(BF16)
    zns = [lax.dot_general(qns[c], kbs[c], _NT, preferred_element_type=F32) for c in range(n)]
    lks, logbs, csums = [], [], []
    for c in range(n):
        zn = zns[c]
        lk = jnp.minimum(zn, 0.0) - jnp.log(1.0 + jnp.exp2(jnp.abs(zn) * (-LOG2_E)))
        logbs.append(lk - zn)
        lks.append(jnp.where(keep, lk, 0.0))
    for c in range(n):
        csums.append(jnp.dot(lks[c].astype(BF16), later, preferred_element_type=F32))
    res = []
    for c in range(n):
        logw = logbs[c] + csums[c]
        if rs is not None:
            logw = logw + rs[c]
        wgt = jnp.where(keep, jnp.exp(logw), 0.0)
        out = jnp.dot(wgt.astype(BF16), vbs[c], preferred_element_type=F32)
        total = jnp.sum(lks[c], axis=-1, keepdims=True)
        res.append((out, total if rs is None else rs[c] + total))
    return res


def _attn_kernel(tq, tk, n_tiles, n_sub, pad, q_ref, k_ref, v_ref, g_ref, o_ref, acc_ref, r_ref):
    hpt = HEADS_PER_LANE_TILE
    lane = lax.broadcasted_iota(jnp.int32, (tq, LANES), 1)
    head_lanes = [(lane >= h * SB_HEAD_DIM) & (lane < (h + 1) * SB_HEAD_DIM) for h in range(hpt)]
    cols = [slice(c * LANES, (c + 1) * LANES) for c in range(n_tiles)]
    q_row = lax.broadcasted_iota(jnp.int32, (tq, tk), 0)
    k_col = lax.broadcasted_iota(jnp.int32, (tq, tk), 1)

    def row_max(res):
        rm = jnp.max(res[0][1])
        for c in range(1, n_tiles):
            rm = jnp.maximum(rm, jnp.max(res[c][1]))
        return rm

    for sub in range(n_sub):
        tile = pl.program_id(2) * n_sub + sub
        q_rows = slice(sub * tq, (sub + 1) * tq)
        qn = []
        for c in range(n_tiles):
            q = -q_ref[0, q_rows, cols[c]]
            qn.append(jnp.concatenate([jnp.where(m, q, jnp.zeros_like(q)) for m in head_lanes], axis=0))

        def sweep(start, rs, causal, qn=qn):
            exists = k_col >= (pad - N_META) - start
            keep = exists if causal is None else causal & exists
            keep = jnp.concatenate([keep] * hpt, axis=0)
            rows = pl.ds(pl.multiple_of(start, tq), tk)
            return _sb_blocks(qn, [k_ref[0, rows, cols[c]] for c in range(n_tiles)],
                              [v_ref[0, rows, cols[c]] for c in range(n_tiles)], rs, keep)

        start0 = pad + tile * tq - (tk - tq)
        res = sweep(start0, None, k_col < q_row + (tk - tq))
        for c in range(n_tiles):
            acc_ref[c] = res[c][0]
            r_ref[c] = res[c][1]

        def cond(carry):
            start, rm = carry
            return (start + tk > pad - N_META) & (rm > F32_EXP_ZERO_BELOW)

        def body(carry, sweep=sweep):
            start, _ = carry
            res = sweep(start, [r_ref[c] for c in range(n_tiles)], None)
            for c in range(n_tiles):
                acc_ref[c] += res[c][0]
                r_ref[c] = res[c][1]
            return start - tk, row_max(res)

        lax.while_loop(cond, body, (start0 - tk, row_max(res)))

        for c in range(n_tiles):
            o = acc_ref[c, 0:tq, :]
            for h in range(1, hpt):
                o = jnp.where(head_lanes[h], acc_ref[c, h * tq:(h + 1) * tq, :], o)
            g = g_ref[0, q_rows, cols[c]]
            o_ref[0, q_rows, cols[c]] = (o * _silu(g)).astype(BF16)


def _attn(q, k, v, g, tq, tk, n_tiles, n_sub, pad):
    bn, seq, width = q.shape
    assert tk % tq == 0 and pad % tq == 0 and pad >= tk and k.shape[1] == pad + seq
    bw = n_tiles * LANES
    grid = (bn, width // bw, seq // (tq * n_sub))
    q_spec = pl.BlockSpec((1, tq * n_sub, bw), lambda b, p, i: (b, i, p))
    kv_spec = pl.BlockSpec((1, pad + seq, bw), lambda b, p, i: (b, 0, p))
    return pl.pallas_call(
        functools.partial(_attn_kernel, tq, tk, n_tiles, n_sub, pad),
        grid=grid,
        in_specs=[q_spec, kv_spec, kv_spec, q_spec],
        out_specs=q_spec,
        out_shape=jax.ShapeDtypeStruct((bn, seq, width), BF16),
        scratch_shapes=[pltpu.VMEM((n_tiles, HEADS_PER_LANE_TILE * tq, LANES), F32),
                        pltpu.VMEM((n_tiles, HEADS_PER_LANE_TILE * tq, 1), F32)],
        compiler_params=_params(3),
        name="stick_breaking",
    )(q, k, v, g)


def _final_kernel(h1_ref, o_ref, wo_ref, fw_ref, out_ref):
    h2 = h1_ref[...] + jnp.dot(o_ref[...], wo_ref[...], preferred_element_type=F32)
    out_ref[...] = _rmsnorm(h2, fw_ref[...])


def _final(h1, o, wo, fw, tile):
    rows = h1.shape[0]
    row_spec = pl.BlockSpec((tile, D_MODEL), lambda i: (i, 0))
    return pl.pallas_call(
        _final_kernel,
        grid=(rows // tile,),
        in_specs=[row_spec, row_spec, _const_spec((D_MODEL, D_MODEL)), _const_spec((1, D_MODEL))],
        out_specs=row_spec,
        out_shape=jax.ShapeDtypeStruct((rows, D_MODEL), F32),
        compiler_params=_params(1),
        name="odd_out_final_norm",
    )(h1, o, wo, fw)


def _pair_lanes(vec, chunk):
    return jnp.repeat(vec.astype(F32).reshape(SSD_PAIRS, HEADS_PER_LANE_TILE), chunk, axis=1)


def _layer0_and_qkv(x3, carry, p, row_tile, lru_tile, ssd_tile, chunk, meta_kv=None):
    lxc, lg, z, xbc, dt, dtp, lru_tail = _even_in(x3, carry["lru_tail"], p, row_tile, chunk)
    ya, lru_h = _lru(lxc, lg, carry["lru_h"], p, lru_tile)
    yb, ssd_tail, ssd_s = _ssd(xbc, z, dt, dtp, carry["ssd_tail"], carry["ssd_s"], p,
                               _pair_lanes(p["dt_bias"], chunk), _pair_lanes(p["a_log"], chunk), ssd_tile, chunk)
    h1, q, k, v, g = _mid(x3, ya, yb, p, row_tile, meta_kv)
    new_carry = {"lru_tail": lru_tail, "lru_h": lru_h, "ssd_tail": ssd_tail, "ssd_s": ssd_s}
    return h1, q, k, v, g, new_carry


def kernel(x, meta, even_norm, even_w_in, lru_conv_w, lru_conv_b, lru_w_a, lru_b_a, lru_w_x, lru_b_x, lru_lambda,
           ssd_conv_w, ssd_conv_b, ssd_dt_bias, ssd_a_log, ssd_d, ssd_norm, even_w_out, odd_norm, odd_w_in,
           odd_w_out, final_norm):
    bsz, seq, _ = x.shape
    n_main = 2 * LRU_WIDTH + SSD_WIDTH + SSD_CONV_DIM
    w_in = even_w_in[0]
    w_dt = w_in[:, n_main:]

    def lane_row(vec):
        return jnp.pad(vec.astype(F32), (0, LANES - vec.shape[0])).reshape(1, LANES)

    p = {
        "even_norm": even_norm[0].reshape(1, -1),
        "w_main": w_in[:, :n_main].astype(BF16),
        "w_dt": jnp.pad(w_dt, ((0, 0), (0, LANES - SSD_HEADS))).astype(BF16),
        "w_dt_even": w_dt[:, 0::2].T.astype(BF16), "w_dt_odd": w_dt[:, 1::2].T.astype(BF16),
        "lru_conv_w": lru_conv_w[0], "lru_conv_b": lru_conv_b[0].reshape(1, -1),
        "lru_w_a": (0.5 * lru_w_a[0]).astype(BF16), "lru_b_a": 0.5 * lru_b_a[0].reshape(1, -1),
        "lru_w_x": (0.5 * lru_w_x[0]).astype(BF16), "lru_b_x": 0.5 * lru_b_x[0].reshape(1, -1),
        "lru_lambda": lru_lambda[0].reshape(1, -1),
        "ssd_conv_w": ssd_conv_w[0], "ssd_conv_b": ssd_conv_b[0].reshape(1, -1),
        "dt_bias": ssd_dt_bias[0], "a_log": ssd_a_log[0],
        "dt_bias_row": lane_row(ssd_dt_bias[0]), "a_log_row": lane_row(ssd_a_log[0]),
        "d_expanded": jnp.repeat(ssd_d[0], SSD_HEAD_DIM).reshape(1, -1),
        "ssd_norm": ssd_norm[0].reshape(1, -1),
        "even_w_out": even_w_out[0].astype(BF16),
        "odd_norm": odd_norm[0].reshape(1, -1),
        "odd_w_in": odd_w_in[0].astype(BF16),
    }
    zero_carry = {
        "lru_tail": jnp.zeros((1, CONV_HALO, LRU_WIDTH), F32), "lru_h": jnp.zeros((1, 1, LRU_WIDTH), F32),
        "ssd_tail": jnp.zeros((1, CONV_HALO, SSD_CONV_DIM), F32),
        "ssd_s": jnp.zeros((1, SSD_GROUPS, SSD_STATE, SSD_GROUP_WIDTH), F32),
    }
    _, _, km, vm, _, carry = _layer0_and_qkv(meta.astype(x.dtype)[None], zero_carry, p,
                                             row_tile=N_META, lru_tile=N_META, ssd_tile=N_META, chunk=N_META)
    row_tile = min(512, seq)
    h1, q, k, v, g, _ = _layer0_and_qkv(x, carry, p, row_tile=row_tile, lru_tile=min(512, seq),
                                        ssd_tile=min(512, seq), chunk=64, meta_kv=(km, vm))
    q_tile = 64
    o = _attn(q, k, v, g, tq=q_tile, tk=256, n_tiles=8, n_sub=min(8, seq // q_tile), pad=row_tile)
    out = _final(h1.reshape(bsz * seq, D_MODEL), o.reshape(bsz * seq, D_MODEL), odd_w_out[0].astype(BF16),
                 final_norm.reshape(1, -1), min(1024, seq))
    return out.reshape(bsz, seq, D_MODEL)
```

```python
import functools

import jax
import jax.numpy as jnp
from jax import lax
from jax.experimental import pallas as pl
from jax.experimental.pallas import tpu as pltpu

F32 = jnp.float32
BF16 = jnp.bfloat16

D_MODEL = 1024
N_META = 16
EPS = 1e-6
CONV_W = 4
LRU_WIDTH = 1024
LRU_BLOCKS = 4
LRU_BLOCK = LRU_WIDTH // LRU_BLOCKS
RG_LRU_C = 8.0
SSD_WIDTH = 1024
SSD_HEAD_DIM = 64
SSD_HEADS = 16
SSD_GROUPS = 2
SSD_HPG = SSD_HEADS // SSD_GROUPS
SSD_STATE = 128
SSD_CONV_DIM = SSD_WIDTH + 2 * SSD_GROUPS * SSD_STATE
SSD_GROUP_WIDTH = SSD_WIDTH // SSD_GROUPS
SB_HEADS = 16
SB_HEAD_DIM = 64
LANES = 128
HEADS_PER_LANE_TILE = LANES // SB_HEAD_DIM
SSD_PAIRS = SSD_HEADS // HEADS_PER_LANE_TILE
SSD_PAIRS_PER_GROUP = SSD_PAIRS // SSD_GROUPS
CONV_HALO = 8
F32_EXP_ZERO_BELOW = -104.0
LOG2_E = 1.4426950408889634
VMEM_LIMIT = 56 * 1024 * 1024

_NT = (((1,), (1,)), ((), ()))
_TN = (((0,), (0,)), ((), ()))
_EXACT = lax.Precision.HIGHEST


def _params(n_grid):
    return pltpu.CompilerParams(dimension_semantics=("arbitrary",) * n_grid,
                                vmem_limit_bytes=VMEM_LIMIT)


def _const_spec(shape):
    zeros = (0,) * len(shape)
    return pl.BlockSpec(shape, lambda *_: zeros)


def _silu(x):
    half = 0.5 * x
    return half * (jnp.tanh(half) + 1.0)


def _softplus(x):
    return jnp.maximum(x, 0.0) + jnp.log(1.0 + jnp.exp(-jnp.abs(x)))


def _rmsnorm(x, w):
    ms = jnp.mean(x * x, axis=-1, keepdims=True)
    return (x * lax.rsqrt(ms + EPS)) * w


def _causal_conv(x, prev, cw, cb):
    assert x.shape[0] > CONV_HALO
    acc = cb + cw[CONV_W - 1:CONV_W, :] * x
    row = lax.broadcasted_iota(jnp.int32, (CONV_HALO, x.shape[1]), 0)
    for back in range(1, CONV_W):
        tap = CONV_W - 1 - back
        rolled = pltpu.roll(x, back, axis=0)
        head = jnp.where(row < back, pltpu.roll(prev, back, axis=0), rolled[0:CONV_HALO, :])
        acc = acc + cw[tap:tap + 1, :] * jnp.concatenate([head, rolled[CONV_HALO:, :]], axis=0)
    return acc


def _even_in_kernel(t, chunk, x_ref, ltail_ref, nw_ref, w_ref, wdt_ref, wdte_ref, wdto_ref, lcw_ref, lcb_ref,
                    lxc_ref, lg_ref, z_ref, xbc_ref, dt_ref, dtp_ref, ltail_out_ref, lbuf):
    i = pl.program_id(1)

    @pl.when(i == 0)
    def _():
        lbuf[...] = ltail_ref[0]

    u = _rmsnorm(x_ref[0], nw_ref[...]).astype(BF16)
    c0, c1, c2, c3 = LRU_WIDTH, 2 * LRU_WIDTH, 2 * LRU_WIDTH + SSD_WIDTH, 2 * LRU_WIDTH + SSD_WIDTH + SSD_CONV_DIM
    strip = 512
    pending = None

    def finish(off, proj):
        cs = slice(off, off + strip)
        lxc_ref[0, :, cs] = _causal_conv(proj, lbuf[:, cs], lcw_ref[:, cs], lcb_ref[:, cs])
        lbuf[:, cs] = proj[t - CONV_HALO:t, :]

    for off in range(0, LRU_WIDTH, strip):
        proj = jnp.dot(u, w_ref[:, off:off + strip], preferred_element_type=F32)
        if pending is not None:
            finish(*pending)
        pending = (off, proj)
    xbc_ref[0] = jnp.dot(u, w_ref[:, c2:c3], preferred_element_type=F32)
    finish(*pending)
    lg_ref[0] = jnp.dot(u, w_ref[:, c0:c1], preferred_element_type=F32)
    z_ref[0] = jnp.dot(u, w_ref[:, c1:c2], preferred_element_type=F32)
    dt_ref[0] = jnp.dot(u, wdt_ref[...], preferred_element_type=F32)
    for c in range(t // chunk):
        uc = u[c * chunk:(c + 1) * chunk, :]
        even = lax.dot_general(wdte_ref[...], uc, _NT, preferred_element_type=F32)
        odd = lax.dot_general(wdto_ref[...], uc, _NT, preferred_element_type=F32)
        dtp_ref[c] = jnp.concatenate([even, odd], axis=1)

    @pl.when(i == pl.num_programs(1) - 1)
    def _():
        ltail_out_ref[0] = lbuf[...]


def _even_in(x3, ltail, p, tile, chunk):
    bn, seq, _ = x3.shape
    nt = seq // tile
    n_main = p["w_main"].shape[1]
    tile_spec = lambda c: pl.BlockSpec((1, tile, c), lambda b, i: (b, i, 0))
    shared = lambda c: pl.BlockSpec((1, CONV_HALO, c), lambda b, i: (0, 0, 0))
    per_b = lambda c: pl.BlockSpec((1, CONV_HALO, c), lambda b, i: (b, 0, 0))
    act = lambda c, dt=F32: jax.ShapeDtypeStruct((bn, seq, c), dt)
    return pl.pallas_call(
        functools.partial(_even_in_kernel, tile, chunk),
        grid=(bn, nt),
        in_specs=[tile_spec(D_MODEL), shared(LRU_WIDTH),
                  _const_spec((1, D_MODEL)), _const_spec((D_MODEL, n_main)), _const_spec((D_MODEL, LANES)),
                  _const_spec((SSD_PAIRS, D_MODEL)), _const_spec((SSD_PAIRS, D_MODEL)),
                  _const_spec((CONV_W, LRU_WIDTH)), _const_spec((1, LRU_WIDTH))],
        out_specs=[tile_spec(LRU_WIDTH), tile_spec(LRU_WIDTH), tile_spec(SSD_WIDTH), tile_spec(SSD_CONV_DIM),
                   tile_spec(LANES),
                   pl.BlockSpec((tile // chunk, SSD_PAIRS, 2 * chunk), lambda b, i: (b * nt + i, 0, 0)),
                   per_b(LRU_WIDTH)],
        out_shape=[act(LRU_WIDTH), act(LRU_WIDTH), act(SSD_WIDTH), act(SSD_CONV_DIM), act(LANES),
                   jax.ShapeDtypeStruct((bn * seq // chunk, SSD_PAIRS, 2 * chunk), F32),
                   jax.ShapeDtypeStruct((bn, CONV_HALO, LRU_WIDTH), F32)],
        scratch_shapes=[pltpu.VMEM((CONV_HALO, LRU_WIDTH), F32)],
        compiler_params=_params(2),
        name="even_in",
    )(x3, ltail, p["even_norm"], p["w_main"], p["w_dt"], p["w_dt_even"], p["w_dt_odd"],
      p["lru_conv_w"], p["lru_conv_b"])


def _lru_kernel(t, lx_ref, lg_ref, h0_ref, wa_ref, ba_ref, wx_ref, bx_ref, lam_ref,
                ya_ref, h_out_ref, abuf, bbuf, hbuf, hcar):
    i = pl.program_id(1)

    @pl.when(i == 0)
    def _():
        hcar[...] = h0_ref[0]

    lx = lx_ref[0]
    lxb = lx.astype(BF16)
    for g in range(LRU_BLOCKS):
        sl = slice(g * LRU_BLOCK, (g + 1) * LRU_BLOCK)
        xg = lxb[:, sl]
        tr = jnp.tanh(jnp.dot(xg, wa_ref[g], preferred_element_type=F32) + ba_ref[:, sl])
        ti = jnp.tanh(jnp.dot(xg, wx_ref[g], preferred_element_type=F32) + bx_ref[:, sl])
        half_c_sp = (-0.5 * RG_LRU_C) * _softplus(-lam_ref[:, sl])
        a = jnp.exp(half_c_sp * tr + half_c_sp)
        mult = jnp.sqrt(1.0 - a * a)
        abuf[:, sl] = a
        bbuf[:, sl] = (mult * (ti + 1.0)) * (0.5 * lx[:, sl])

    def step(row, h):
        h = abuf[pl.ds(row, 1), :] * h + bbuf[pl.ds(row, 1), :]
        hbuf[pl.ds(row, 1), :] = h
        return h

    h = lax.fori_loop(0, t, step, hcar[...], unroll=8)
    hcar[...] = h
    ya_ref[0] = (hbuf[...] * _silu(lg_ref[0])).astype(BF16)

    @pl.when(i == pl.num_programs(1) - 1)
    def _():
        h_out_ref[0] = h


def _lru(lxc, lg, h0, p, tile):
    bn, seq, _ = lxc.shape
    tile_spec = pl.BlockSpec((1, tile, LRU_WIDTH), lambda b, i: (b, i, 0))
    return pl.pallas_call(
        functools.partial(_lru_kernel, tile),
        grid=(bn, seq // tile),
        in_specs=[tile_spec, tile_spec, pl.BlockSpec((1, 1, LRU_WIDTH), lambda b, i: (0, 0, 0)),
                  _const_spec((LRU_BLOCKS, LRU_BLOCK, LRU_BLOCK)), _const_spec((1, LRU_WIDTH)),
                  _const_spec((LRU_BLOCKS, LRU_BLOCK, LRU_BLOCK)), _const_spec((1, LRU_WIDTH)),
                  _const_spec((1, LRU_WIDTH))],
        out_specs=[tile_spec, pl.BlockSpec((1, 1, LRU_WIDTH), lambda b, i: (b, 0, 0))],
        out_shape=[jax.ShapeDtypeStruct((bn, seq, LRU_WIDTH), BF16),
                   jax.ShapeDtypeStruct((bn, 1, LRU_WIDTH), F32)],
        scratch_shapes=[pltpu.VMEM((tile, LRU_WIDTH), F32), pltpu.VMEM((tile, LRU_WIDTH), F32),
                        pltpu.VMEM((tile, LRU_WIDTH), F32), pltpu.VMEM((1, LRU_WIDTH), F32)],
        compiler_params=_params(2),
        name="rg_lru",
    )(lxc, lg, h0, p["lru_w_a"], p["lru_b_a"], p["lru_w_x"], p["lru_b_x"], p["lru_lambda"])


def _ssd_kernel(t, q, xbc_ref, z_ref, dt_ref, dtp_ref, tail_ref, s0_ref, cw_ref, cb_ref, dtb_ref, dtbp_ref,
                alog_ref, alogp_ref, dexp_ref, nrm_ref, yb_ref, tail_out_ref, s_out_ref, ybuf, xsbuf, sbuf, tbuf):
    i = pl.program_id(1)

    @pl.when(i == 0)
    def _():
        sbuf[...] = s0_ref[0]
        tbuf[...] = tail_ref[0]

    a_row = -jnp.exp(alog_ref[...])
    a_pk = -jnp.exp(alogp_ref[...])
    rr = lax.broadcasted_iota(jnp.int32, (q, q), 0)
    cc = lax.broadcasted_iota(jnp.int32, (q, q), 1)
    tril = (cc <= rr).astype(F32)
    r2 = lax.broadcasted_iota(jnp.int32, (2 * q, 2 * q), 0)
    c2 = lax.broadcasted_iota(jnp.int32, (2 * q, 2 * q), 1)
    bd = ((r2 <= c2) & ((r2 < q) == (c2 < q))).astype(F32)
    lane_x = lax.broadcasted_iota(jnp.int32, (q, LANES), 1)
    first_x = lane_x < SSD_HEAD_DIM
    lane_m = lax.broadcasted_iota(jnp.int32, (q, 2 * q), 1)
    first_m = lane_m < q
    row_m = lax.broadcasted_iota(jnp.int32, (q, 2 * q), 0)
    causal_m = jnp.where(first_m, lane_m, lane_m - q) <= row_m

    n_par = min(2, t // q)

    def chunks(step, carry):
        ks = range(n_par)
        cidx = [step * n_par + k for k in ks]
        rows = [pl.ds(pl.multiple_of(cidx[k] * q, q), q) for k in ks]
        xc = []
        for k in ks:
            before = xbc_ref[0, pl.ds(pl.multiple_of(jnp.maximum(cidx[k] * q - CONV_HALO, 0), CONV_HALO),
                                      CONV_HALO), :]
            prev = jnp.where(cidx[k] == 0, tbuf[...], before)
            xc.append(_silu(_causal_conv(xbc_ref[0, rows[k], :], prev, cw_ref[...], cb_ref[...])))
            xsbuf[rows[k], :] = xc[k][:, 0:SSD_WIDTH]
        dt = [_softplus(dt_ref[0, rows[k], :] + dtb_ref[...]) for k in ks]
        dtp = [_softplus(dtp_ref[cidx[k]] + dtbp_ref[...]) for k in ks]
        acum = [jnp.dot(tril, dt[k] * a_row, precision=_EXACT, preferred_element_type=F32) for k in ks]
        acum_pk = [jnp.dot(dtp[k] * a_pk, bd, precision=_EXACT, preferred_element_type=F32) for k in ks]
        b_g, c_g = [[None] * SSD_GROUPS for _ in ks], [[None] * SSD_GROUPS for _ in ks]
        y_diag, wx, e_acum, e_tot = ([[None] * SSD_PAIRS for _ in ks] for _ in range(4))
        for g in range(SSD_GROUPS):
            b_lo = SSD_WIDTH + g * SSD_STATE
            c_lo = SSD_WIDTH + (SSD_GROUPS + g) * SSD_STATE
            cbb = []
            for k in ks:
                b_g[k][g] = xc[k][:, b_lo:b_lo + SSD_STATE].astype(BF16)
                c_g[k][g] = xc[k][:, c_lo:c_lo + SSD_STATE].astype(BF16)
                cbb.append(lax.dot_general(c_g[k][g], jnp.concatenate([b_g[k][g]] * 2, axis=0), _NT,
                                           preferred_element_type=F32))
            for pp in range(SSD_PAIRS_PER_GROUP):
                pr = g * SSD_PAIRS_PER_GROUP + pp
                h0, h1 = 2 * pr, 2 * pr + 1
                for k in ks:
                    acol_x = jnp.where(first_x, acum[k][:, h0:h0 + 1], acum[k][:, h1:h1 + 1])
                    dt_x = jnp.where(first_x, dt[k][:, h0:h0 + 1], dt[k][:, h1:h1 + 1])
                    if 2 * q == LANES:
                        acol_m = acol_x
                    else:
                        acol_m = jnp.where(first_m, acum[k][:, h0:h0 + 1], acum[k][:, h1:h1 + 1])
                    xdt = xc[k][:, pr * LANES:(pr + 1) * LANES] * dt_x
                    xdt_b = xdt.astype(BF16)
                    decay = jnp.where(causal_m, jnp.exp(acol_m - acum_pk[k][pr:pr + 1, :]), 0.0)
                    m = (cbb[k] * decay).astype(BF16)
                    zeros = jnp.zeros_like(xdt_b)
                    rhs = jnp.concatenate([jnp.where(first_x, xdt_b, zeros), jnp.where(first_x, zeros, xdt_b)],
                                          axis=0)
                    y_diag[k][pr] = jnp.dot(m, rhs, preferred_element_type=F32)
                    tot_x = acol_x[q - 1:q, :]
                    wx[k][pr] = (jnp.exp(tot_x - acol_x) * xdt).astype(BF16)
                    e_acum[k][pr] = jnp.exp(acol_x)
                    e_tot[k][pr] = jnp.exp(tot_x)
        for k in ks:
            for g in range(SSD_GROUPS):
                prs = slice(g * SSD_PAIRS_PER_GROUP, (g + 1) * SSD_PAIRS_PER_GROUP)
                cols = slice(g * SSD_GROUP_WIDTH, (g + 1) * SSD_GROUP_WIDTH)
                state = sbuf[g]
                y_off = jnp.dot(c_g[k][g], state.astype(BF16), preferred_element_type=F32)
                ybuf[rows[k], cols] = (jnp.concatenate(y_diag[k][prs], axis=1)
                                       + y_off * jnp.concatenate(e_acum[k][prs], axis=1))
                sbuf[g] = jnp.concatenate(e_tot[k][prs], axis=1) * state + lax.dot_general(
                    b_g[k][g], jnp.concatenate(wx[k][prs], axis=1), _TN, preferred_element_type=F32)
        return carry

    lax.fori_loop(0, t // (q * n_par), chunks, 0)

    y = ybuf[...] + xsbuf[...] * dexp_ref[...]
    tbuf[...] = xbc_ref[0, t - CONV_HALO:t, :]
    zz = z_ref[0]
    gt = y * _silu(zz)
    for g in range(SSD_GROUPS):
        cols = slice(g * SSD_GROUP_WIDTH, (g + 1) * SSD_GROUP_WIDTH)
        gg = gt[:, cols]
        ms = jnp.mean(gg * gg, axis=-1, keepdims=True)
        yb_ref[0, :, cols] = (gg * lax.rsqrt(ms + EPS) * nrm_ref[:, cols]).astype(BF16)

    @pl.when(i == pl.num_programs(1) - 1)
    def _():
        s_out_ref[0] = sbuf[...]
        tail_out_ref[0] = tbuf[...]


def _ssd(xbc, z, dt, dtp, tail, s0, p, dtbp, alogp, tile, chunk):
    bn, seq, _ = xbc.shape
    nt = seq // tile
    tile_spec = lambda c: pl.BlockSpec((1, tile, c), lambda b, i: (b, i, 0))
    state_shape = (SSD_GROUPS, SSD_STATE, SSD_GROUP_WIDTH)
    return pl.pallas_call(
        functools.partial(_ssd_kernel, tile, chunk),
        grid=(bn, nt),
        in_specs=[tile_spec(SSD_CONV_DIM), tile_spec(SSD_WIDTH), tile_spec(LANES),
                  pl.BlockSpec((tile // chunk, SSD_PAIRS, 2 * chunk), lambda b, i: (b * nt + i, 0, 0)),
                  pl.BlockSpec((1, CONV_HALO, SSD_CONV_DIM), lambda b, i: (0, 0, 0)),
                  pl.BlockSpec((1,) + state_shape, lambda b, i: (0, 0, 0, 0)),
                  _const_spec((CONV_W, SSD_CONV_DIM)), _const_spec((1, SSD_CONV_DIM)),
                  _const_spec((1, LANES)), _const_spec((SSD_PAIRS, 2 * chunk)),
                  _const_spec((1, LANES)), _const_spec((SSD_PAIRS, 2 * chunk)),
                  _const_spec((1, SSD_WIDTH)), _const_spec((1, SSD_WIDTH))],
        out_specs=[tile_spec(SSD_WIDTH), pl.BlockSpec((1, CONV_HALO, SSD_CONV_DIM), lambda b, i: (b, 0, 0)),
                   pl.BlockSpec((1,) + state_shape, lambda b, i: (b, 0, 0, 0))],
        out_shape=[jax.ShapeDtypeStruct((bn, seq, SSD_WIDTH), BF16),
                   jax.ShapeDtypeStruct((bn, CONV_HALO, SSD_CONV_DIM), F32),
                   jax.ShapeDtypeStruct((bn,) + state_shape, F32)],
        scratch_shapes=[pltpu.VMEM((tile, SSD_WIDTH), F32), pltpu.VMEM((tile, SSD_WIDTH), F32),
                        pltpu.VMEM(state_shape, F32), pltpu.VMEM((CONV_HALO, SSD_CONV_DIM), F32)],
        compiler_params=_params(2),
        name="ssd",
    )(xbc, z, dt, dtp, tail, s0, p["ssd_conv_w"], p["ssd_conv_b"], p["dt_bias_row"], dtbp, p["a_log_row"], alogp,
      p["d_expanded"], p["ssd_norm"])


def _mid_body(x_ref, ya_ref, yb_ref, wo_ref, nw_ref, wi_ref, h1_ref, q_ref, k_ref, v_ref, g_ref):
    h1 = (x_ref[0] + jnp.dot(ya_ref[0], wo_ref[0:LRU_WIDTH, :], preferred_element_type=F32)
          + jnp.dot(yb_ref[0], wo_ref[LRU_WIDTH:LRU_WIDTH + SSD_WIDTH, :], preferred_element_type=F32))
    h1_ref[0] = h1
    u = _rmsnorm(h1, nw_ref[...]).astype(BF16)
    w = D_MODEL
    q_ref[0] = (jnp.dot(u, wi_ref[:, 0:w], preferred_element_type=F32) * (SB_HEAD_DIM ** -0.5)).astype(BF16)
    k_ref[0] = jnp.dot(u, wi_ref[:, w:2 * w], preferred_element_type=F32).astype(BF16)
    v_ref[0] = jnp.dot(u, wi_ref[:, 2 * w:3 * w], preferred_element_type=F32).astype(BF16)
    g_ref[0] = jnp.dot(u, wi_ref[:, 3 * w:4 * w], preferred_element_type=F32)


def _mid_front_kernel(tile, x_ref, ya_ref, yb_ref, wo_ref, nw_ref, wi_ref, km_ref, vm_ref,
                      h1_ref, q_ref, k_ref, v_ref, g_ref):
    i = pl.program_id(1)

    @pl.when(i == 0)
    def _():
        for dst, src in ((k_ref, km_ref), (v_ref, vm_ref)):
            dst[0, 0:tile - N_META, :] = jnp.zeros((tile - N_META, D_MODEL), BF16)
            dst[0, tile - N_META:tile, :] = src[0]

    @pl.when(i > 0)
    def _():
        _mid_body(x_ref, ya_ref, yb_ref, wo_ref, nw_ref, wi_ref, h1_ref, q_ref, k_ref, v_ref, g_ref)


def _mid(x3, ya, yb, p, tile, meta_kv=None):
    bn, seq, _ = x3.shape
    nt = seq // tile
    act = lambda dt, rows=seq: jax.ShapeDtypeStruct((bn, rows, D_MODEL), dt)
    weights = [_const_spec((LRU_WIDTH + SSD_WIDTH, D_MODEL)), _const_spec((1, D_MODEL)),
               _const_spec((D_MODEL, 4 * D_MODEL))]
    args = [x3, ya, yb, p["even_w_out"], p["odd_norm"], p["odd_w_in"]]
    if meta_kv is None:
        tile_spec = pl.BlockSpec((1, tile, D_MODEL), lambda b, i: (b, i, 0))
        return pl.pallas_call(
            _mid_body,
            grid=(bn, nt),
            in_specs=[tile_spec] * 3 + weights,
            out_specs=[tile_spec] * 5,
            out_shape=[act(F32), act(BF16), act(BF16), act(BF16), act(F32)],
            compiler_params=_params(2),
            name="even_out_odd_in",
        )(*args)
    tile_spec = pl.BlockSpec((1, tile, D_MODEL), lambda b, i: (b, jnp.maximum(i - 1, 0), 0))
    kv_spec = pl.BlockSpec((1, tile, D_MODEL), lambda b, i: (b, i, 0))
    meta_spec = pl.BlockSpec((1, N_META, D_MODEL), lambda b, i: (0, 0, 0))
    return pl.pallas_call(
        functools.partial(_mid_front_kernel, tile),
        grid=(bn, nt + 1),
        in_specs=[tile_spec] * 3 + weights + [meta_spec] * 2,
        out_specs=[tile_spec, tile_spec, kv_spec, kv_spec, tile_spec],
        out_shape=[act(F32), act(BF16), act(BF16, tile + seq), act(BF16, tile + seq), act(F32)],
        compiler_params=_params(2),
        name="even_out_odd_in",
    )(*args, *meta_kv)


def _sb_blocks(qns, kbs, vbs, rs, keep, later):
    n = len(qns)
    zns = [lax.dot_general(qns[c], kbs[c], _NT, preferred_element_type=F32) for c in range(n)]
    lks, logbs, csums = [], [], []
    for c in range(n):
        zn = zns[c]
        lk = jnp.minimum(zn, 0.0) - jnp.log(1.0 + jnp.exp2(jnp.abs(zn) * (-LOG2_E)))
        logbs.append(lk - zn)
        lks.append(jnp.where(keep, lk, 0.0))
    for c in range(n):
        csums.append(jnp.dot(lks[c].astype(BF16), later, preferred_element_type=F32))
    res = []
    for c in range(n):
        logw = logbs[c] + csums[c]
        if rs is not None:
            logw = logw + rs[c]
        wgt = jnp.where(keep, jnp.exp(logw), 0.0)
        out = jnp.dot(wgt.astype(BF16), vbs[c], preferred_element_type=F32)
        total = jnp.sum(lks[c], axis=-1, keepdims=True)
        res.append((out, total if rs is None else rs[c] + total))
    return res


def _attn_kernel(tq, tk, n_tiles, n_sub, pad, q_ref, k_ref, v_ref, g_ref, o_ref, acc_ref, r_ref):
    hpt = HEADS_PER_LANE_TILE
    lane = lax.broadcasted_iota(jnp.int32, (tq, LANES), 1)
    head_lanes = [(lane >= h * SB_HEAD_DIM) & (lane < (h + 1) * SB_HEAD_DIM) for h in range(hpt)]
    cols = [slice(c * LANES, (c + 1) * LANES) for c in range(n_tiles)]
    q_row = lax.broadcasted_iota(jnp.int32, (tq, tk), 0)
    k_col = lax.broadcasted_iota(jnp.int32, (tq, tk), 1)
    later = (lax.broadcasted_iota(jnp.int32, (tk, tk), 0)
             > lax.broadcasted_iota(jnp.int32, (tk, tk), 1)).astype(BF16)

    def row_max(res):
        rm = jnp.max(res[0][1])
        for c in range(1, n_tiles):
            rm = jnp.maximum(rm, jnp.max(res[c][1]))
        return rm

    for sub in range(n_sub):
        tile = pl.program_id(2) * n_sub + sub
        q_rows = slice(sub * tq, (sub + 1) * tq)
        qn = []
        for c in range(n_tiles):
            q = -q_ref[0, q_rows, cols[c]]
            qn.append(jnp.concatenate([jnp.where(m, q, jnp.zeros_like(q)) for m in head_lanes], axis=0))

        def sweep(start, rs, causal, qn=qn):
            exists = k_col >= (pad - N_META) - start
            keep = exists if causal is None else causal & exists
            keep = jnp.concatenate([keep] * hpt, axis=0)
            rows = pl.ds(pl.multiple_of(start, tq), tk)
            return _sb_blocks(qn, [k_ref[0, rows, cols[c]] for c in range(n_tiles)],
                              [v_ref[0, rows, cols[c]] for c in range(n_tiles)], rs, keep, later)

        start0 = pad + tile * tq - (tk - tq)
        res = sweep(start0, None, k_col < q_row + (tk - tq))
        for c in range(n_tiles):
            acc_ref[c] = res[c][0]
            r_ref[c] = res[c][1]

        def cond(carry):
            start, rm = carry
            return (start + tk > pad - N_META) & (rm > F32_EXP_ZERO_BELOW)

        def body(carry, sweep=sweep):
            start, _ = carry
            res = sweep(start, [r_ref[c] for c in range(n_tiles)], None)
            for c in range(n_tiles):
                acc_ref[c] += res[c][0]
                r_ref[c] = res[c][1]
            return start - tk, row_max(res)

        lax.while_loop(cond, body, (start0 - tk, row_max(res)))

        for c in range(n_tiles):
            o = acc_ref[c, 0:tq, :]
            for h in range(1, hpt):
                o = jnp.where(head_lanes[h], acc_ref[c, h * tq:(h + 1) * tq, :], o)
            g = g_ref[0, q_rows, cols[c]]
            o_ref[0, q_rows, cols[c]] = (o * _silu(g)).astype(BF16)


def _attn(q, k, v, g, tq, tk, n_tiles, n_sub, pad):
    bn, seq, width = q.shape
    assert tk % tq == 0 and pad % tq == 0 and pad >= tk and k.shape[1] == pad + seq
    bw = n_tiles * LANES
    grid = (bn, width // bw, seq // (tq * n_sub))
    q_spec = pl.BlockSpec((1, tq * n_sub, bw), lambda b, p, i: (b, i, p))
    kv_spec = pl.BlockSpec((1, pad + seq, bw), lambda b, p, i: (b, 0, p))
    return pl.pallas_call(
        functools.partial(_attn_kernel, tq, tk, n_tiles, n_sub, pad),
        grid=grid,
        in_specs=[q_spec, kv_spec, kv_spec, q_spec],
        out_specs=q_spec,
        out_shape=jax.ShapeDtypeStruct((bn, seq, width), BF16),
        scratch_shapes=[pltpu.VMEM((n_tiles, HEADS_PER_LANE_TILE * tq, LANES), F32),
                        pltpu.VMEM((n_tiles, HEADS_PER_LANE_TILE * tq, 1), F32)],
        compiler_params=_params(3),
        name="stick_breaking",
    )(q, k, v, g)


def _final_kernel(h1_ref, o_ref, wo_ref, fw_ref, out_ref):
    h2 = h1_ref[...] + jnp.dot(o_ref[...], wo_ref[...], preferred_element_type=F32)
    out_ref[...] = _rmsnorm(h2, fw_ref[...])


def _final(h1, o, wo, fw, tile):
    rows = h1.shape[0]
    row_spec = pl.BlockSpec((tile, D_MODEL), lambda i: (i, 0))
    return pl.pallas_call(
        _final_kernel,
        grid=(rows // tile,),
        in_specs=[row_spec, row_spec, _const_spec((D_MODEL, D_MODEL)), _const_spec((1, D_MODEL))],
        out_specs=row_spec,
        out_shape=jax.ShapeDtypeStruct((rows, D_MODEL), F32),
        compiler_params=_params(1),
        name="odd_out_final_norm",
    )(h1, o, wo, fw)


def _pair_lanes(vec, chunk):
    return jnp.repeat(vec.astype(F32).reshape(SSD_PAIRS, HEADS_PER_LANE_TILE), chunk, axis=1)


def _layer0_and_qkv(x3, carry, p, row_tile, lru_tile, ssd_tile, chunk, meta_kv=None):
    lxc, lg, z, xbc, dt, dtp, lru_tail = _even_in(x3, carry["lru_tail"], p, row_tile, chunk)
    ya, lru_h = _lru(lxc, lg, carry["lru_h"], p, lru_tile)
    yb, ssd_tail, ssd_s = _ssd(xbc, z, dt, dtp, carry["ssd_tail"], carry["ssd_s"], p,
                               _pair_lanes(p["dt_bias"], chunk), _pair_lanes(p["a_log"], chunk), ssd_tile, chunk)
    h1, q, k, v, g = _mid(x3, ya, yb, p, row_tile, meta_kv)
    new_carry = {"lru_tail": lru_tail, "lru_h": lru_h, "ssd_tail": ssd_tail, "ssd_s": ssd_s}
    return h1, q, k, v, g, new_carry


def kernel(x, meta, even_norm, even_w_in, lru_conv_w, lru_conv_b, lru_w_a, lru_b_a, lru_w_x, lru_b_x, lru_lambda,
           ssd_conv_w, ssd_conv_b, ssd_dt_bias, ssd_a_log, ssd_d, ssd_norm, even_w_out, odd_norm, odd_w_in,
           odd_w_out, final_norm):
    bsz, seq, _ = x.shape
    n_main = 2 * LRU_WIDTH + SSD_WIDTH + SSD_CONV_DIM
    w_in = even_w_in[0]
    w_dt = w_in[:, n_main:]

    def lane_row(vec):
        return jnp.pad(vec.astype(F32), (0, LANES - vec.shape[0])).reshape(1, LANES)

    p = {
        "even_norm": even_norm[0].reshape(1, -1),
        "w_main": w_in[:, :n_main].astype(BF16),
        "w_dt": jnp.pad(w_dt, ((0, 0), (0, LANES - SSD_HEADS))).astype(BF16),
        "w_dt_even": w_dt[:, 0::2].T.astype(BF16), "w_dt_odd": w_dt[:, 1::2].T.astype(BF16),
        "lru_conv_w": lru_conv_w[0], "lru_conv_b": lru_conv_b[0].reshape(1, -1),
        "lru_w_a": (0.5 * lru_w_a[0]).astype(BF16), "lru_b_a": 0.5 * lru_b_a[0].reshape(1, -1),
        "lru_w_x": (0.5 * lru_w_x[0]).astype(BF16), "lru_b_x": 0.5 * lru_b_x[0].reshape(1, -1),
        "lru_lambda": lru_lambda[0].reshape(1, -1),
        "ssd_conv_w": ssd_conv_w[0], "ssd_conv_b": ssd_conv_b[0].reshape(1, -1),
        "dt_bias": ssd_dt_bias[0], "a_log": ssd_a_log[0],
        "dt_bias_row": lane_row(ssd_dt_bias[0]), "a_log_row": lane_row(ssd_a_log[0]),
        "d_expanded": jnp.repeat(ssd_d[0], SSD_HEAD_DIM).reshape(1, -1),
        "ssd_norm": ssd_norm[0].reshape(1, -1),
        "even_w_out": even_w_out[0].astype(BF16),
        "odd_norm": odd_norm[0].reshape(1, -1),
        "odd_w_in": odd_w_in[0].astype(BF16),
    }
    zero_carry = {
        "lru_tail": jnp.zeros((1, CONV_HALO, LRU_WIDTH), F32), "lru_h": jnp.zeros((1, 1, LRU_WIDTH), F32),
        "ssd_tail": jnp.zeros((1, CONV_HALO, SSD_CONV_DIM), F32),
        "ssd_s": jnp.zeros((1, SSD_GROUPS, SSD_STATE, SSD_GROUP_WIDTH), F32),
    }
    _, _, km, vm, _, carry = _layer0_and_qkv(meta.astype(x.dtype)[None], zero_carry, p,
                                             row_tile=N_META, lru_tile=N_META, ssd_tile=N_META, chunk=N_META)
    row_tile = min(512, seq)
    h1, q, k, v, g, _ = _layer0_and_qkv(x, carry, p, row_tile=row_tile, lru_tile=min(1024, seq),
                                        ssd_tile=min(1024, seq), chunk=64, meta_kv=(km, vm))
    q_tile = 64
    o = _attn(q, k, v, g, tq=q_tile, tk=256, n_tiles=8, n_sub=min(8, seq // q_tile), pad=row_tile)
    out = _final(h1.reshape(bsz * seq, D_MODEL), o.reshape(bsz * seq, D_MODEL), odd_w_out[0].astype(BF16),
                 final_norm.reshape(1, -1), min(1024, seq))
    return out.reshape(bsz, seq, D_MODEL)
```
